```python
import math
import jax, jax.numpy as jnp
from jax import lax
import numpy as np

D_MODEL = 1024
BATCH = 2
SEQ = 16384
DEPTH = 1
DEC_BATCH = 4
DEC_SEQ = 8192
PAST_LEN = 128

SSM_WIDTH = D_MODEL // 2
SSM_GROUP_CH = 16
SSM_GROUPS = SSM_WIDTH // SSM_GROUP_CH
SSM_STATE = 64
DT_MIN = 1e-3
DT_MAX = 1e-1
N_HEADS = 8
QK_NOPE = 64
QK_ROPE = 32
V_HEAD = 64
Q_LORA = D_MODEL // 4
KV_LORA = D_MODEL // 8
ATTN_WIDTH = N_HEADS * V_HEAD
ROPE_THETA = 10000.0
Q_BLOCK = 128
LN_EPS = 1e-5
RMS_EPS = 1e-6
ALPHA = (2.0 * DEPTH) ** 0.25
BETA = (8.0 * DEPTH) ** -0.25

IN_SIZES = (SSM_WIDTH, SSM_WIDTH, Q_LORA, KV_LORA, QK_ROPE, ATTN_WIDTH, D_MODEL, D_MODEL)
IN_WIDTH = sum(IN_SIZES)
IN_SPLITS = tuple(int(v) for v in np.cumsum(IN_SIZES)[:-1])

kernel_name = 'hybrid_s5_mla_gated_encoder'

F32 = jnp.float32


def _layer_norm(x, g, b):
    xf = x.astype(F32)
    mu = jnp.mean(xf, -1, keepdims=True)
    var = jnp.mean(jnp.square(xf - mu), -1, keepdims=True)
    y = (xf - mu) * lax.rsqrt(var + LN_EPS) * g.astype(F32) + b.astype(F32)
    return y.astype(x.dtype)


def _rms_norm(x, g):
    xf = x.astype(F32)
    y = xf * lax.rsqrt(jnp.mean(xf * xf, -1, keepdims=True) + RMS_EPS) * g.astype(F32)
    return y.astype(x.dtype)


def _rope(x, pos):
    half = x.shape[-1] // 2
    inv = ROPE_THETA ** (-jnp.arange(half, dtype=F32) * 2.0 / x.shape[-1])
    ang = pos[:, None] * inv[None, :]
    cos = jnp.cos(ang)[:, None, :]
    sin = jnp.sin(ang)[:, None, :]
    xf = x.astype(F32)
    x1, x2 = xf[..., :half], xf[..., half:]
    return jnp.concatenate([x1 * cos - x2 * sin, x1 * sin + x2 * cos], -1).astype(x.dtype)


def _linear_scan_op(left, right):
    a_l, b_l = left
    a_r, b_r = right
    return a_l * a_r, a_r * b_l + b_r


def _s5_direction(bu, a_re, a_im, log_dt, c_re, c_im, reverse):
    lam = lax.complex(a_re.astype(F32), a_im.astype(F32))
    dt = jnp.exp(log_dt.astype(F32))[:, None]
    lam_bar = jnp.exp(lam * dt)
    b_seq = bu * ((lam_bar - 1.0) / lam)
    a_seq = jnp.broadcast_to(lam_bar, b_seq.shape)
    _, h = lax.associative_scan(_linear_scan_op, (a_seq, b_seq), axis=1, reverse=reverse)
    c = lax.complex(c_re.astype(F32), c_im.astype(F32))
    return jnp.real(jnp.einsum('blgn,gpn->blgp', h, c))


def _s5_branch(u, b_re, b_im, a_re_fwd, a_im_fwd, log_dt_fwd, c_re_fwd, c_im_fwd,
               a_re_bwd, a_im_bwd, log_dt_bwd, c_re_bwd, c_im_bwd, d_skip, w_glu, b_glu):
    bsz, L, _ = u.shape
    ug = u.astype(F32).reshape(bsz, L, SSM_GROUPS, SSM_GROUP_CH)
    bu = lax.complex(jnp.einsum('blgp,gnp->blgn', ug, b_re.astype(F32)),
                     jnp.einsum('blgp,gnp->blgn', ug, b_im.astype(F32)))
    y = (_s5_direction(bu, a_re_fwd, a_im_fwd, log_dt_fwd, c_re_fwd, c_im_fwd, False)
         + _s5_direction(bu, a_re_bwd, a_im_bwd, log_dt_bwd, c_re_bwd, c_im_bwd, True)
         + d_skip.astype(F32).reshape(SSM_GROUPS, SSM_GROUP_CH) * ug)
    y = jax.nn.gelu(y.reshape(bsz, L, SSM_WIDTH)).astype(u.dtype)
    return y * jax.nn.sigmoid(y @ w_glu + b_glu)


def _mla_branch(c_q, c_kv, k_rope_in, q_norm_g, w_uq, kv_norm_g, w_ukv):
    bsz, L, _ = c_q.shape
    pos = jnp.arange(L, dtype=F32)
    q = (_rms_norm(c_q, q_norm_g) @ w_uq).reshape(bsz, L, N_HEADS, QK_NOPE + QK_ROPE)
    q_nope = q[..., :QK_NOPE]
    q_rope = _rope(q[..., QK_NOPE:], pos)
    kv = (_rms_norm(c_kv, kv_norm_g) @ w_ukv).reshape(bsz, L, N_HEADS, QK_NOPE + V_HEAD)
    k_nope = kv[..., :QK_NOPE]
    v = kv[..., QK_NOPE:]
    k_rope = _rope(k_rope_in[:, :, None, :], pos)[:, :, 0, :]
    scale = (QK_NOPE + QK_ROPE) ** -0.5
    n_blk = L // Q_BLOCK
    qn_blocks = q_nope.reshape(bsz, n_blk, Q_BLOCK, N_HEADS, QK_NOPE).transpose(1, 0, 2, 3, 4)
    qr_blocks = q_rope.reshape(bsz, n_blk, Q_BLOCK, N_HEADS, QK_ROPE).transpose(1, 0, 2, 3, 4)

    def attend(blk):
        qn_b, qr_b = blk
        s = (jnp.einsum('bqhd,bkhd->bhqk', qn_b, k_nope, preferred_element_type=F32)
             + jnp.einsum('bqhr,bkr->bhqk', qr_b, k_rope, preferred_element_type=F32))
        p = jax.nn.softmax(s * scale, axis=-1)
        return jnp.einsum('bhqk,bkhd->bqhd', p.astype(v.dtype), v)

    out = lax.map(attend, (qn_blocks, qr_blocks))
    return out.transpose(1, 0, 2, 3, 4).reshape(bsz, L, ATTN_WIDTH)


def _layer(x, w_in, ssm_b_re, ssm_b_im, ssm_a_re_fwd, ssm_a_im_fwd, ssm_log_dt_fwd,
           ssm_c_re_fwd, ssm_c_im_fwd, ssm_a_re_bwd, ssm_a_im_bwd, ssm_log_dt_bwd,
           ssm_c_re_bwd, ssm_c_im_bwd, ssm_d, w_glu, b_glu, q_norm_g, w_uq, kv_norm_g,
           w_ukv, w_branch_ssm, w_branch_attn, w_o, ln_g, ln_b):
    h = x @ w_in
    u, z_s, c_q, c_kv, k_r, z_a, g_s, g_a = jnp.split(h, IN_SPLITS, axis=-1)
    y_s = _s5_branch(u, ssm_b_re, ssm_b_im, ssm_a_re_fwd, ssm_a_im_fwd, ssm_log_dt_fwd,
                     ssm_c_re_fwd, ssm_c_im_fwd, ssm_a_re_bwd, ssm_a_im_bwd, ssm_log_dt_bwd,
                     ssm_c_re_bwd, ssm_c_im_bwd, ssm_d, w_glu, b_glu) * jax.nn.silu(z_s)
    y_a = _mla_branch(c_q, c_kv, k_r, q_norm_g, w_uq, kv_norm_g, w_ukv) * jax.nn.silu(z_a)
    merged = jax.nn.sigmoid(g_s) * (y_s @ w_branch_ssm) + jax.nn.sigmoid(g_a) * (y_a @ w_branch_attn)
    return _layer_norm(ALPHA * x + merged @ w_o, ln_g, ln_b)


def _trunk(x, ln_in_g, ln_in_b, weights):
    x = _layer_norm(x, ln_in_g, ln_in_b)
    for l in range(DEPTH):
        x = _layer(x, *[w[l] for w in weights])
    return x


def setup_inputs(seed: int = 0) -> dict:
    key = jax.random.key(seed)
    ks = jax.random.split(key, 40)

    def nrm(k, shape, scale):
        return jax.random.normal(k, shape, F32) * scale

    G, N, P = SSM_GROUPS, SSM_STATE, SSM_GROUP_CH
    n_idx = jnp.arange(N, dtype=F32)

    def a_re(k):
        return -0.5 + nrm(k, (DEPTH, G, N), 0.01)

    def a_im(k):
        return jnp.pi * n_idx + nrm(k, (DEPTH, G, N), 0.01)

    def log_dt(k):
        return jax.random.uniform(k, (DEPTH, G), F32, math.log(DT_MIN), math.log(DT_MAX))

    return {
        'x_prompt': nrm(ks[0], (BATCH, SEQ, D_MODEL), 1.0),
        'x_sample': nrm(ks[1], (DEC_BATCH, DEC_SEQ, D_MODEL), 1.0),
        'ln_in_g': 1.0 + nrm(ks[2], (D_MODEL,), 0.01),
        'ln_in_b': nrm(ks[3], (D_MODEL,), 0.01),
        'w_in': nrm(ks[4], (DEPTH, D_MODEL, IN_WIDTH), D_MODEL ** -0.5),
        'ssm_b_re': nrm(ks[5], (DEPTH, G, N, P), (2.0 * P) ** -0.5),
        'ssm_b_im': nrm(ks[6], (DEPTH, G, N, P), (2.0 * P) ** -0.5),
        'ssm_a_re_fwd': a_re(ks[7]),
        'ssm_a_im_fwd': a_im(ks[8]),
        'ssm_log_dt_fwd': log_dt(ks[9]),
        'ssm_c_re_fwd': nrm(ks[10], (DEPTH, G, P, N), (2.0 * N) ** -0.5),
        'ssm_c_im_fwd': nrm(ks[11], (DEPTH, G, P, N), (2.0 * N) ** -0.5),
        'ssm_a_re_bwd': a_re(ks[12]),
        'ssm_a_im_bwd': a_im(ks[13]),
        'ssm_log_dt_bwd': log_dt(ks[14]),
        'ssm_c_re_bwd': nrm(ks[15], (DEPTH, G, P, N), (2.0 * N) ** -0.5),
        'ssm_c_im_bwd': nrm(ks[16], (DEPTH, G, P, N), (2.0 * N) ** -0.5),
        'ssm_d': nrm(ks[17], (DEPTH, SSM_WIDTH), 1.0),
        'w_glu': nrm(ks[18], (DEPTH, SSM_WIDTH, SSM_WIDTH), SSM_WIDTH ** -0.5),
        'b_glu': nrm(ks[19], (DEPTH, SSM_WIDTH), 0.01),
        'q_norm_g': 1.0 + nrm(ks[20], (DEPTH, Q_LORA), 0.01),
        'w_uq': nrm(ks[21], (DEPTH, Q_LORA, N_HEADS * (QK_NOPE + QK_ROPE)), Q_LORA ** -0.5),
        'kv_norm_g': 1.0 + nrm(ks[22], (DEPTH, KV_LORA), 0.01),
        'w_ukv': nrm(ks[23], (DEPTH, KV_LORA, N_HEADS * (QK_NOPE + V_HEAD)), KV_LORA ** -0.5),
        'w_branch_ssm': nrm(ks[24], (DEPTH, SSM_WIDTH, D_MODEL), BETA * SSM_WIDTH ** -0.5),
        'w_branch_attn': nrm(ks[25], (DEPTH, ATTN_WIDTH, D_MODEL), BETA * ATTN_WIDTH ** -0.5),
        'w_o': nrm(ks[26], (DEPTH, D_MODEL, D_MODEL), BETA * D_MODEL ** -0.5),
        'ln_g': 1.0 + nrm(ks[27], (DEPTH, D_MODEL), 0.01),
        'ln_b': nrm(ks[28], (DEPTH, D_MODEL), 0.01),
    }


def reference(x_prompt, x_sample, ln_in_g, ln_in_b, w_in, ssm_b_re, ssm_b_im,
              ssm_a_re_fwd, ssm_a_im_fwd, ssm_log_dt_fwd, ssm_c_re_fwd, ssm_c_im_fwd,
              ssm_a_re_bwd, ssm_a_im_bwd, ssm_log_dt_bwd, ssm_c_re_bwd, ssm_c_im_bwd,
              ssm_d, w_glu, b_glu, q_norm_g, w_uq, kv_norm_g, w_ukv,
              w_branch_ssm, w_branch_attn, w_o, ln_g, ln_b):
    weights = (w_in, ssm_b_re, ssm_b_im, ssm_a_re_fwd, ssm_a_im_fwd, ssm_log_dt_fwd,
               ssm_c_re_fwd, ssm_c_im_fwd, ssm_a_re_bwd, ssm_a_im_bwd, ssm_log_dt_bwd,
               ssm_c_re_bwd, ssm_c_im_bwd, ssm_d, w_glu, b_glu, q_norm_g, w_uq, kv_norm_g,
               w_ukv, w_branch_ssm, w_branch_attn, w_o, ln_g, ln_b)
    y_prompt = _trunk(x_prompt, ln_in_g, ln_in_b, weights)
    y_sample = _trunk(x_sample, ln_in_g, ln_in_b, weights)
    return (y_prompt, y_sample)
```

```python
import functools
import math

import jax
import jax.numpy as jnp
from jax import lax
from jax.experimental import pallas as pl
from jax.experimental.pallas import tpu as pltpu

F32 = jnp.float32
BF16 = jnp.bfloat16
HIGHEST = lax.Precision.HIGHEST

D_MODEL = 1024
DEPTH = 1
SSM_WIDTH = 512
SSM_P = 16
SSM_G = 32
SSM_N = 64
N_HEADS = 8
QK_NOPE = 64
QK_ROPE = 32
V_HEAD = 64
Q_LORA = 256
KV_LORA = 128
ATTN_WIDTH = N_HEADS * V_HEAD
ROPE_THETA = 10000.0
LN_EPS = 1e-5
RMS_EPS = 1e-6
ALPHA = (2.0 * DEPTH) ** 0.25
IN_SIZES = (SSM_WIDTH, SSM_WIDTH, Q_LORA, KV_LORA, QK_ROPE, ATTN_WIDTH, D_MODEL, D_MODEL)

LANES = 128
SUBLANES = 8
S5_PW_ROWS = 16
VMEM_LIMIT_BYTES = 48 * 1024 * 1024

SSM_CHUNK = 16
SSM_PAIR = 2 * SSM_CHUNK * SSM_P
PRE_TILE = 512
ATTN_TQ = 128
ATTN_TK = 512
QK_WIDTH = 2 * LANES
HALF_ROPE = QK_ROPE // 2


def _dot(a, b):
    return jnp.dot(a, b, preferred_element_type=F32)


def _layer_norm(x, g, b):
    mu = jnp.mean(x, axis=-1, keepdims=True)
    xc = x - mu
    var = jnp.mean(xc * xc, axis=-1, keepdims=True)
    return xc * lax.rsqrt(var + LN_EPS) * g + b


def _rms_norm(x, g):
    return x * lax.rsqrt(jnp.mean(x * x, axis=-1, keepdims=True) + RMS_EPS) * g


def _sigmoid(x):
    return 1.0 / (1.0 + jnp.exp(-x))


def _pre_kernel(x_ref, lng_ref, lnb_ref, wu_ref, wzs_ref, wza_ref, wgs_ref, wga_ref,
                wcq_ref, wckv_ref, wkr_ref, qng_ref, wq_ref, kvng_ref, inv_ref, sgn_ref,
                u_ref, zs_ref, za_ref, gs_ref, ga_ref, q_ref, k_ref, *, seq_tiles):
    tm = x_ref.shape[0]
    xb = _layer_norm(x_ref[...], lng_ref[...], lnb_ref[...]).astype(BF16)
    u_ref[...] = _dot(xb, wu_ref[...]).astype(BF16)
    zs_ref[...] = _dot(xb, wzs_ref[...]).astype(BF16)
    za_ref[...] = _dot(xb, wza_ref[...]).astype(BF16)
    gs_ref[...] = _dot(xb, wgs_ref[...]).astype(BF16)
    ga_ref[...] = _dot(xb, wga_ref[...]).astype(BF16)

    pos0 = (pl.program_id(0) % seq_tiles) * tm
    pos = (pos0 + lax.broadcasted_iota(jnp.int32, (tm, LANES), 0)).astype(F32)
    ang = pos * inv_ref[...]
    cs = jnp.cos(ang)
    sn = jnp.sin(ang) * sgn_ref[...]

    def rope(a, b):
        return a * cs + b * sn

    cqn = _rms_norm(_dot(xb, wcq_ref[...]), qng_ref[...]).astype(BF16)
    qf = _dot(cqn, wq_ref[...])
    for h in range(N_HEADS):
        base = 3 * LANES * h
        lat = qf[:, base:base + LANES]
        rot = rope(qf[:, base + LANES:base + 2 * LANES], qf[:, base + 2 * LANES:base + 3 * LANES])
        q_ref[0, h] = jnp.concatenate([lat, rot], axis=1).astype(BF16)

    kvn = _rms_norm(_dot(xb, wckv_ref[...]), kvng_ref[...])
    kr = _dot(xb, wkr_ref[...])
    krot = rope(kr[:, :LANES], kr[:, LANES:])
    k_ref[0] = jnp.concatenate([kvn, krot], axis=1).astype(BF16)


def _pre_call(x2d, bsz, seq, w):
    tokens = x2d.shape[0]
    tm = min(PRE_TILE, seq)
    seq_tiles = seq // tm
    n_tiles = tokens // tm
    row = lambda i: (i, 0)
    const = lambda i: (0, 0)
    full = lambda a: pl.BlockSpec(a.shape, const)
    weights = [w['ln_in_g'], w['ln_in_b'], w['w_u'], w['w_zs'], w['w_za'], w['w_gs'], w['w_ga'],
               w['w_cq'], w['w_ckv'], w['w_kr'], w['q_norm_g'], w['w_q'], w['kv_norm_g'],
               w['rope_inv'], w['rope_sgn']]
    out_shape = (
        jax.ShapeDtypeStruct((tokens, SSM_WIDTH), BF16),
        jax.ShapeDtypeStruct((tokens, SSM_WIDTH), BF16),
        jax.ShapeDtypeStruct((tokens, ATTN_WIDTH), BF16),
        jax.ShapeDtypeStruct((tokens, D_MODEL), BF16),
        jax.ShapeDtypeStruct((tokens, D_MODEL), BF16),
        jax.ShapeDtypeStruct((bsz, N_HEADS, seq, QK_WIDTH), BF16),
        jax.ShapeDtypeStruct((bsz, seq, QK_WIDTH), BF16),
    )
    out_specs = (
        pl.BlockSpec((tm, SSM_WIDTH), row),
        pl.BlockSpec((tm, SSM_WIDTH), row),
        pl.BlockSpec((tm, ATTN_WIDTH), row),
        pl.BlockSpec((tm, D_MODEL), row),
        pl.BlockSpec((tm, D_MODEL), row),
        pl.BlockSpec((1, N_HEADS, tm, QK_WIDTH), lambda i: (i // seq_tiles, 0, i % seq_tiles, 0)),
        pl.BlockSpec((1, tm, QK_WIDTH), lambda i: (i // seq_tiles, i % seq_tiles, 0)),
    )
    return pl.pallas_call(
        functools.partial(_pre_kernel, seq_tiles=seq_tiles),
        grid=(n_tiles,),
        in_specs=[pl.BlockSpec((tm, D_MODEL), row)] + [full(a) for a in weights],
        out_specs=out_specs,
        out_shape=out_shape,
        compiler_params=pltpu.CompilerParams(
            dimension_semantics=("arbitrary",), vmem_limit_bytes=VMEM_LIMIT_BYTES),
        name="pre",
    )(x2d, *weights)


def _cmul_add(ar, ai, xr, xi, yr, yi):
    return ar * xr - ai * xi + yr, ar * xi + ai * xr + yi


def _tile_scan(sr, si, cr, ci, pw, reverse):
    row = lax.broadcasted_iota(jnp.int32, (SUBLANES, LANES), 0)
    xr, xi = sr, si
    for k, s in enumerate((1, 2, 4)):
        shift = SUBLANES - s if reverse else s
        keep = (row < SUBLANES - s) if reverse else (row >= s)
        shr = jnp.where(keep, pltpu.roll(xr, shift, 0), 0.0)
        shi = jnp.where(keep, pltpu.roll(xi, shift, 0), 0.0)
        ar, ai = pw(SUBLANES + k)
        xr, xi = _cmul_add(ar, ai, shr, shi, xr, xi)
    pr, pi = pw(slice(0, SUBLANES))
    hr, hi = _cmul_add(pr, pi, cr, ci, xr, xi)
    edge = 0 if reverse else SUBLANES - 1
    out_r = jnp.broadcast_to(hr[edge:edge + 1], (SUBLANES, LANES))
    out_i = jnp.broadcast_to(hi[edge:edge + 1], (SUBLANES, LANES))
    first = (row == SUBLANES - 1) if reverse else (row == 0)
    one = SUBLANES - 1 if reverse else 1
    in_r = jnp.where(first, cr, pltpu.roll(hr, one, 0))
    in_i = jnp.where(first, ci, pltpu.roll(hi, one, 0))
    return in_r, in_i, out_r, out_i


def _s5_kernel(u_ref, t_ref, win_ref, wout_ref, pw_ref, y_ref, s_sc, h_sc, *, nb, nchunk):
    u = u_ref[0]
    s_sc[...] = _dot(u, win_ref[0])
    ntile = nchunk // SUBLANES

    def table(direction):
        base = direction * 2 * S5_PW_ROWS
        def pw(k):
            k = slice(k, k + 1) if isinstance(k, int) else k
            re = pw_ref[0, base + k.start:base + k.stop, :]
            im = pw_ref[0, base + S5_PW_ROWS + k.start:base + S5_PW_ROWS + k.stop, :]
            return re, im
        return pw

    pw_f, pw_b = table(0), table(1)

    def step(t, carry):
        new = []
        for b in range(nb):
            cfr, cfi, cbr, cbi = carry[b]
            rf = pl.multiple_of(b * nchunk + t * SUBLANES, SUBLANES)
            rb = pl.multiple_of(b * nchunk + (ntile - 1 - t) * SUBLANES, SUBLANES)
            rows_f, rows_b = pl.ds(rf, SUBLANES), pl.ds(rb, SUBLANES)
            in_r, in_i, cfr, cfi = _tile_scan(s_sc[rows_f, 0:LANES], s_sc[rows_f, LANES:2 * LANES],
                                              cfr, cfi, pw_f, False)
            h_sc[rows_f, 0:LANES] = in_r
            h_sc[rows_f, LANES:2 * LANES] = in_i
            in_r, in_i, cbr, cbi = _tile_scan(s_sc[rows_b, 2 * LANES:3 * LANES], s_sc[rows_b, 3 * LANES:4 * LANES],
                                              cbr, cbi, pw_b, True)
            h_sc[rows_b, 2 * LANES:3 * LANES] = in_r
            h_sc[rows_b, 3 * LANES:4 * LANES] = in_i
            new.append((cfr, cfi, cbr, cbi))
        return tuple(new)

    zero = jnp.zeros((SUBLANES, LANES), F32)
    lax.fori_loop(0, ntile, step, tuple((zero, zero, zero, zero) for _ in range(nb)))
    y_ref[0] = _dot(u, t_ref[0]) + _dot(h_sc[...].astype(BF16), wout_ref[0])


def _s5_call(u2, nb, nchunk, w):
    npair, rows, width = u2.shape
    blk = lambda i: (i, 0, 0)
    return pl.pallas_call(
        functools.partial(_s5_kernel, nb=nb, nchunk=nchunk),
        grid=(npair,),
        in_specs=[pl.BlockSpec((1, rows, width), blk),
                  pl.BlockSpec((1, width, width), blk),
                  pl.BlockSpec((1, width, width), blk),
                  pl.BlockSpec((1, width, width), blk),
                  pl.BlockSpec((1, 4 * S5_PW_ROWS, LANES), blk)],
        out_specs=pl.BlockSpec((1, rows, width), blk),
        out_shape=jax.ShapeDtypeStruct((npair, rows, width), F32),
        scratch_shapes=[pltpu.VMEM((rows, width), F32), pltpu.VMEM((rows, width), F32)],
        compiler_params=pltpu.CompilerParams(
            dimension_semantics=("arbitrary",), vmem_limit_bytes=VMEM_LIMIT_BYTES),
        name="s5",
    )(u2, w['s5_t'], w['s5_win'], w['s5_wout'], w['s5_pw'])


def _attn_kernel(q_ref, k_ref, wuv_ref, o_ref, m_sc, l_sc, acc_sc, *, tk):
    nh, tq, width = q_ref.shape[1:]
    rows = nh * tq
    nk = k_ref.shape[1] // tk
    q = q_ref[0].reshape(rows, width)
    m_sc[...] = jnp.full(m_sc.shape, -1e30, F32)
    l_sc[...] = jnp.zeros(l_sc.shape, F32)
    acc_sc[...] = jnp.zeros(acc_sc.shape, F32)

    def body(kt, carry):
        kblk = k_ref[0, pl.ds(pl.multiple_of(kt * tk, tk), tk), :]
        s = lax.dot_general(q, kblk, (((1,), (1,)), ((), ())), preferred_element_type=F32)
        m_prev = m_sc[...]
        m_next = jnp.maximum(m_prev, jnp.max(s, axis=1, keepdims=True))
        alpha = jnp.exp2(m_prev - m_next)
        p = jnp.exp2(s - jnp.tile(m_next, (1, tk // LANES)))
        l_sc[...] = alpha * l_sc[...] + jnp.sum(p, axis=1, keepdims=True)
        acc_sc[...] = alpha * acc_sc[...] + _dot(p.astype(BF16), kblk[:, :LANES])
        m_sc[...] = m_next
        return carry

    lax.fori_loop(0, nk, body, 0)
    lat = (acc_sc[...] / l_sc[...]).astype(BF16)
    outs = [_dot(lat[h * tq:(h + 1) * tq], wuv_ref[h]) for h in range(nh)]
    o_ref[0] = jnp.concatenate(outs, axis=1)


def _attn_call(q, k, wuv):
    bsz, nh, seq, width = q.shape
    tq = min(ATTN_TQ, seq)
    tk = min(ATTN_TK, seq)
    rows = nh * tq
    return pl.pallas_call(
        functools.partial(_attn_kernel, tk=tk),
        grid=(bsz, seq // tq),
        in_specs=[pl.BlockSpec((1, nh, tq, width), lambda b, i: (b, 0, i, 0)),
                  pl.BlockSpec((1, seq, width), lambda b, i: (b, 0, 0)),
                  pl.BlockSpec(wuv.shape, lambda b, i: (0, 0, 0))],
        out_specs=pl.BlockSpec((1, tq, ATTN_WIDTH), lambda b, i: (b, i, 0)),
        out_shape=jax.ShapeDtypeStruct((bsz, seq, ATTN_WIDTH), F32),
        scratch_shapes=[pltpu.VMEM((rows, LANES), F32), pltpu.VMEM((rows, LANES), F32),
                        pltpu.VMEM((rows, LANES), F32)],
        compiler_params=pltpu.CompilerParams(
            dimension_semantics=("arbitrary", "arbitrary"), vmem_limit_bytes=VMEM_LIMIT_BYTES),
        name="attn",
    )(q, k, wuv)


def _post_kernel(x_ref, ys_ref, zs_ref, at_ref, za_ref, gs_ref, ga_ref, lig_ref, lib_ref,
                 wglu_ref, bglu_ref, wbs_ref, wba_ref, wo_ref, lng_ref, lnb_ref, o_ref):
    xln = _layer_norm(x_ref[...], lig_ref[...], lib_ref[...])
    ys = jax.nn.gelu(ys_ref[...])
    ys = ys * _sigmoid(_dot(ys.astype(BF16), wglu_ref[...]) + bglu_ref[...])
    zs = zs_ref[...].astype(F32)
    ys = ys * (zs * _sigmoid(zs))
    za = za_ref[...].astype(F32)
    ya = at_ref[...] * (za * _sigmoid(za))
    merged = (_sigmoid(gs_ref[...].astype(F32)) * _dot(ys.astype(BF16), wbs_ref[...])
              + _sigmoid(ga_ref[...].astype(F32)) * _dot(ya.astype(BF16), wba_ref[...]))
    r = ALPHA * xln + _dot(merged.astype(BF16), wo_ref[...])
    o_ref[...] = _layer_norm(r, lng_ref[...], lnb_ref[...])


def _post_call(x2d, ys, zs, at, za, gs, ga, w):
    tokens = x2d.shape[0]
    tm = min(PRE_TILE, tokens)
    row = lambda i: (i, 0)
    const = lambda i: (0, 0)
    full = lambda a: pl.BlockSpec(a.shape, const)
    acts = [x2d, ys, zs, at, za, gs, ga]
    weights = [w['ln_in_g'], w['ln_in_b'], w['w_glu'], w['b_glu'], w['w_bs'], w['w_ba'], w['w_o'],
               w['ln_g'], w['ln_b']]
    return pl.pallas_call(
        _post_kernel,
        grid=(tokens // tm,),
        in_specs=[pl.BlockSpec((tm, a.shape[1]), row) for a in acts] + [full(a) for a in weights],
        out_specs=pl.BlockSpec((tm, D_MODEL), row),
        out_shape=jax.ShapeDtypeStruct((tokens, D_MODEL), F32),
        compiler_params=pltpu.CompilerParams(
            dimension_semantics=("arbitrary",), vmem_limit_bytes=VMEM_LIMIT_BYTES),
        name="post",
    )(*acts, *weights)


def _s5_operators(p):
    q = SSM_CHUNK
    g, n, pch = SSM_G, SSM_N, SSM_P
    k = jnp.arange(q + 1, dtype=F32)

    def direction(a_re, a_im, log_dt, c_re, c_im):
        dt = jnp.exp(log_dt)[:, None]
        er = jnp.exp(a_re * dt)
        lbr, lbi = er * jnp.cos(a_im * dt), er * jnp.sin(a_im * dt)
        den = a_re * a_re + a_im * a_im
        nr, ni = lbr - 1.0, lbi
        cfr = (nr * a_re + ni * a_im) / den
        cfi = (ni * a_re - nr * a_im) / den
        btr = p['b_re'] * cfr[..., None] - p['b_im'] * cfi[..., None]
        bti = p['b_re'] * cfi[..., None] + p['b_im'] * cfr[..., None]
        mag = jnp.exp(k[:, None, None] * (a_re * dt)[None])
        ph = k[:, None, None] * (a_im * dt)[None]
        pwr, pwi = mag * jnp.cos(ph), mag * jnp.sin(ph)
        cpr = c_re[None] * pwr[:, :, None, :] - c_im[None] * pwi[:, :, None, :]
        cpi = c_re[None] * pwi[:, :, None, :] + c_im[None] * pwr[:, :, None, :]
        taps = (jnp.einsum('tgpn,gnq->tgpq', cpr, btr, precision=HIGHEST)
                - jnp.einsum('tgpn,gnq->tgpq', cpi, bti, precision=HIGHEST))
        pbr = pwr[..., None] * btr[None] - pwi[..., None] * bti[None]
        pbi = pwr[..., None] * bti[None] + pwi[..., None] * btr[None]
        return taps, (pbr, pbi), (cpr, cpi), (pwr[q], pwi[q])

    tf, pbf, cpf, af = direction(p['a_re_f'], p['a_im_f'], p['log_dt_f'], p['c_re_f'], p['c_im_f'])
    tb, pbb, cpb, ab = direction(p['a_re_b'], p['a_im_b'], p['log_dt_b'], p['c_re_b'], p['c_im_b'])

    ii = jnp.arange(q)[None, :]
    jj = jnp.arange(q)[:, None]
    lag_f = jnp.clip(ii - jj, 0, q)
    lag_b = jnp.clip(jj - ii, 0, q)
    kf = tf[lag_f] * (jj <= ii)[..., None, None, None]
    kb = tb[lag_b] * (jj >= ii)[..., None, None, None]
    dsk = p['d_skip'].reshape(g, pch)
    eye = (jj == ii)[..., None, None, None] * (jnp.eye(pch, dtype=F32)[None, None, None] * dsk[None, None, :, :, None])
    t_op = jnp.transpose(kf + kb + eye, (2, 0, 4, 1, 3)).reshape(g, q * pch, q * pch)

    idx_f = q - 1 - jnp.arange(q)
    idx_b = jnp.arange(q)
    def win_part(x, idx):
        return jnp.transpose(x[idx], (1, 0, 3, 2)).reshape(g, q * pch, n)
    win = [win_part(pbf[0], idx_f), win_part(pbf[1], idx_f), win_part(pbb[0], idx_b), win_part(pbb[1], idx_b)]

    odx_f = jnp.arange(q) + 1
    odx_b = q - jnp.arange(q)
    def wout_part(x, idx, sign):
        return sign * jnp.transpose(x[idx], (1, 3, 0, 2)).reshape(g, n, q * pch)
    wout = [wout_part(cpf[0], odx_f, 1.0), wout_part(cpf[1], odx_f, -1.0),
            wout_part(cpb[0], odx_b, 1.0), wout_part(cpb[1], odx_b, -1.0)]

    gp = g // 2
    w1 = q * pch
    t2 = jnp.zeros((gp, 2, w1, 2, w1), F32)
    win2 = jnp.zeros((gp, 2, w1, 4, 2, n), F32)
    wout2 = jnp.zeros((gp, 4, 2, n, 2, w1), F32)
    for e in range(2):
        t2 = t2.at[:, e, :, e, :].set(t_op[e::2])
        for c in range(4):
            win2 = win2.at[:, e, :, c, e, :].set(win[c][e::2])
            wout2 = wout2.at[:, c, e, :, e, :].set(wout[c][e::2])
    def powers(a, reverse):
        ar, ai = a
        pows = [(ar, ai)]
        for _ in range(SUBLANES - 1):
            pr, pi = pows[-1]
            pows.append((pr * ar - pi * ai, pr * ai + pi * ar))
        per_row = pows[::-1] if reverse else pows
        rows = per_row + [pows[0], pows[1], pows[3]]
        rows = rows + [(jnp.zeros_like(ar), jnp.zeros_like(ar))] * (S5_PW_ROWS - len(rows))
        re = jnp.stack([r[0] for r in rows], axis=1)
        im = jnp.stack([r[1] for r in rows], axis=1)
        both = jnp.stack([re, im], axis=1)
        both = both.reshape(gp, 2, 2, S5_PW_ROWS, n)
        return jnp.transpose(both, (0, 2, 3, 1, 4)).reshape(gp, 2 * S5_PW_ROWS, 2 * n)

    pw2 = jnp.concatenate([powers(af, False), powers(ab, True)], axis=1)
    return (t2.reshape(gp, 2 * w1, 2 * w1).astype(BF16),
            win2.reshape(gp, 2 * w1, 8 * n).astype(BF16),
            wout2.reshape(gp, 8 * n, 2 * w1).astype(BF16),
            pw2)


def _prep_weights(ln_in_g, ln_in_b, w_in, ssm_b_re, ssm_b_im, ssm_a_re_fwd, ssm_a_im_fwd,
                  ssm_log_dt_fwd, ssm_c_re_fwd, ssm_c_im_fwd, ssm_a_re_bwd, ssm_a_im_bwd,
                  ssm_log_dt_bwd, ssm_c_re_bwd, ssm_c_im_bwd, ssm_d, w_glu, b_glu, q_norm_g, w_uq,
                  kv_norm_g, w_ukv, w_branch_ssm, w_branch_attn, w_o, ln_g, ln_b):
    l = 0
    w = {}
    row = lambda v: v.reshape(1, -1).astype(F32)
    w['ln_in_g'], w['ln_in_b'] = row(ln_in_g), row(ln_in_b)
    splits = [0]
    for s in IN_SIZES:
        splits.append(splits[-1] + s)
    seg = lambda i: w_in[l][:, splits[i]:splits[i + 1]]
    w['w_u'], w['w_zs'] = seg(0).astype(BF16), seg(1).astype(BF16)
    w['w_cq'], w['w_ckv'] = seg(2).astype(BF16), seg(3).astype(BF16)
    kr = seg(4)
    pad = jnp.zeros((D_MODEL, LANES - QK_ROPE), F32)
    k1, k2 = kr[:, :HALF_ROPE], kr[:, HALF_ROPE:]
    w['w_kr'] = jnp.concatenate([k1, k2, pad, k2, k1, pad], axis=1).astype(BF16)
    w['w_za'], w['w_gs'], w['w_ga'] = seg(5).astype(BF16), seg(6).astype(BF16), seg(7).astype(BF16)

    wuq = w_uq[l].reshape(Q_LORA, N_HEADS, QK_NOPE + QK_ROPE)
    wukv = w_ukv[l].reshape(KV_LORA, N_HEADS, QK_NOPE + V_HEAD)
    w_lat = jnp.einsum('chd,khd->chk', wuq[:, :, :QK_NOPE], wukv[:, :, :QK_NOPE], precision=HIGHEST)
    r1 = wuq[:, :, QK_NOPE:QK_NOPE + HALF_ROPE]
    r2 = wuq[:, :, QK_NOPE + HALF_ROPE:]
    zpad = jnp.zeros((Q_LORA, N_HEADS, LANES - QK_ROPE), F32)
    qscale = (QK_NOPE + QK_ROPE) ** -0.5 * math.log2(math.e)
    wq = jnp.concatenate([w_lat, r1, r2, zpad, r2, r1, zpad], axis=2) * qscale
    w['w_q'] = wq.reshape(Q_LORA, N_HEADS * 3 * LANES).astype(BF16)
    w['w_uv'] = jnp.transpose(wukv[:, :, QK_NOPE:], (1, 0, 2)).astype(BF16)
    w['q_norm_g'], w['kv_norm_g'] = row(q_norm_g[l]), row(kv_norm_g[l])

    inv = ROPE_THETA ** (-jnp.arange(HALF_ROPE, dtype=F32) * 2.0 / QK_ROPE)
    zl = jnp.zeros((LANES - QK_ROPE,), F32)
    w['rope_inv'] = jnp.concatenate([inv, inv, zl]).reshape(1, LANES)
    one = jnp.ones((HALF_ROPE,), F32)
    w['rope_sgn'] = jnp.concatenate([-one, one, zl]).reshape(1, LANES)

    p = dict(b_re=ssm_b_re[l], b_im=ssm_b_im[l], d_skip=ssm_d[l],
             a_re_f=ssm_a_re_fwd[l], a_im_f=ssm_a_im_fwd[l], log_dt_f=ssm_log_dt_fwd[l],
             c_re_f=ssm_c_re_fwd[l], c_im_f=ssm_c_im_fwd[l],
             a_re_b=ssm_a_re_bwd[l], a_im_b=ssm_a_im_bwd[l], log_dt_b=ssm_log_dt_bwd[l],
             c_re_b=ssm_c_re_bwd[l], c_im_b=ssm_c_im_bwd[l])
    w['s5_t'], w['s5_win'], w['s5_wout'], w['s5_pw'] = _s5_operators(p)

    w['w_glu'], w['b_glu'] = w_glu[l].astype(BF16), row(b_glu[l])
    w['w_bs'], w['w_ba'] = w_branch_ssm[l].astype(BF16), w_branch_attn[l].astype(BF16)
    w['w_o'] = w_o[l].astype(BF16)
    w['ln_g'], w['ln_b'] = row(ln_g[l]), row(ln_b[l])
    return w


def _trunk(x, w):
    bsz, seq, _ = x.shape
    tokens = bsz * seq
    x2d = x.reshape(tokens, D_MODEL)
    u, zs, za, gs, ga, q, k = _pre_call(x2d, bsz, seq, w)

    nchunk = seq // SSM_CHUNK
    rows = bsz * nchunk
    gp = SSM_G // 2
    u2 = u.reshape(rows, SSM_CHUNK, gp, 2, SSM_P)
    u2 = jnp.transpose(u2, (2, 0, 3, 1, 4)).reshape(gp, rows, SSM_PAIR)
    y2 = _s5_call(u2, bsz, nchunk, w)
    ys = jnp.transpose(y2.reshape(gp, rows, 2, SSM_CHUNK, SSM_P), (1, 3, 0, 2, 4))
    ys = ys.reshape(tokens, SSM_WIDTH)

    at = _attn_call(q, k, w['w_uv']).reshape(tokens, ATTN_WIDTH)
    out = _post_call(x2d, ys, zs, at, za, gs, ga, w)
    return out.reshape(bsz, seq, D_MODEL)


def kernel(x_prompt, x_sample, ln_in_g, ln_in_b, w_in, ssm_b_re, ssm_b_im, ssm_a_re_fwd, ssm_a_im_fwd, ssm_log_dt_fwd, ssm_c_re_fwd, ssm_c_im_fwd, ssm_a_re_bwd, ssm_a_im_bwd, ssm_log_dt_bwd, ssm_c_re_bwd, ssm_c_im_bwd, ssm_d, w_glu, b_glu, q_norm_g, w_uq, kv_norm_g, w_ukv, w_branch_ssm, w_branch_attn, w_o, ln_g, ln_b):
    w = _prep_weights(ln_in_g, ln_in_b, w_in, ssm_b_re, ssm_b_im, ssm_a_re_fwd, ssm_a_im_fwd,
                      ssm_log_dt_fwd, ssm_c_re_fwd, ssm_c_im_fwd, ssm_a_re_bwd, ssm_a_im_bwd,
                      ssm_log_dt_bwd, ssm_c_re_bwd, ssm_c_im_bwd, ssm_d, w_glu, b_glu, q_norm_g,
                      w_uq, kv_norm_g, w_ukv, w_branch_ssm, w_branch_attn, w_o, ln_g, ln_b)
    return (_trunk(x_prompt, w), _trunk(x_sample, w))
```

```python
import functools
import math

import jax
import jax.numpy as jnp
from jax import lax
from jax.experimental import pallas as pl
from jax.experimental.pallas import tpu as pltpu

F32 = jnp.float32
BF16 = jnp.bfloat16
HIGHEST = lax.Precision.HIGHEST

D_MODEL = 1024
DEPTH = 1
SSM_WIDTH = 512
SSM_P = 16
SSM_G = 32
SSM_N = 64
N_HEADS = 8
QK_NOPE = 64
QK_ROPE = 32
V_HEAD = 64
Q_LORA = 256
KV_LORA = 128
ATTN_WIDTH = N_HEADS * V_HEAD
ROPE_THETA = 10000.0
LN_EPS = 1e-5
RMS_EPS = 1e-6
ALPHA = (2.0 * DEPTH) ** 0.25
IN_SIZES = (SSM_WIDTH, SSM_WIDTH, Q_LORA, KV_LORA, QK_ROPE, ATTN_WIDTH, D_MODEL, D_MODEL)

LANES = 128
SUBLANES = 8
S5_PW_ROWS = 16
VMEM_LIMIT_BYTES = 48 * 1024 * 1024

SSM_CHUNK = 16
SSM_PAIR = 2 * SSM_CHUNK * SSM_P
PAIR_LANES = 2 * SSM_P
PAIRS_PER_VREG = LANES // PAIR_LANES
PRE_TILE = 512
ATTN_TQ = 128
ATTN_TK = 1024
QK_WIDTH = 2 * LANES
HALF_ROPE = QK_ROPE // 2


def _dot(a, b):
    return jnp.dot(a, b, preferred_element_type=F32)


def _layer_norm(x, g, b):
    mu = jnp.mean(x, axis=-1, keepdims=True)
    xc = x - mu
    var = jnp.mean(xc * xc, axis=-1, keepdims=True)
    return xc * lax.rsqrt(var + LN_EPS) * g + b


def _rms_norm(x, g):
    return x * lax.rsqrt(jnp.mean(x * x, axis=-1, keepdims=True) + RMS_EPS) * g


def _sigmoid(x):
    return 1.0 / (1.0 + jnp.exp(-x))


def _pair_slot(rows):
    return lax.broadcasted_iota(jnp.int32, (rows, LANES), 1) // PAIR_LANES


def _merge_slots(pieces, shifts, slot):
    out = None
    for k, (x, sh) in enumerate(zip(pieces, shifts)):
        x = pltpu.roll(x, sh % LANES, 1) if sh % LANES else x
        out = x if out is None else jnp.where(slot == k, x, out)
    return out


def _to_chunk_major(u, u_sc, u2_ref):
    rows = u.shape[0] // SSM_CHUNK
    slot = _pair_slot(rows)
    n = PAIRS_PER_VREG
    for b in range(SSM_WIDTH // LANES):
        u_sc[b] = u[:, b * LANES:(b + 1) * LANES]
        for v in range(SSM_PAIR // LANES):
            xs = [u_sc[b, pl.ds(n * v + s, rows, stride=SSM_CHUNK), :] for s in range(n)]
            for q in range(n):
                blk = _merge_slots(xs, [(s - q) * PAIR_LANES for s in range(n)], slot)
                u2_ref[n * b + q, :, v * LANES:(v + 1) * LANES] = blk.astype(u2_ref.dtype)


def _to_token_major(y2_ref, y_sc):
    rows = y2_ref.shape[1]
    slot = _pair_slot(rows)
    n = PAIRS_PER_VREG
    for b in range(SSM_WIDTH // LANES):
        for v in range(SSM_PAIR // LANES):
            ys = [y2_ref[n * b + q, :, v * LANES:(v + 1) * LANES] for q in range(n)]
            for s in range(n):
                blk = _merge_slots(ys, [(q - s) * PAIR_LANES for q in range(n)], slot)
                y_sc[b, pl.ds(n * v + s, rows, stride=SSM_CHUNK), :] = blk
    return jnp.concatenate([y_sc[b] for b in range(SSM_WIDTH // LANES)], axis=1)


def _pre_kernel(x_ref, lng_ref, lnb_ref, wu_ref, wzs_ref, wza_ref, wgs_ref, wga_ref,
                wcq_ref, wckv_ref, wkr_ref, qng_ref, wq_ref, kvng_ref, inv_ref, sgn_ref,
                u2_ref, zs_ref, za_ref, gs_ref, ga_ref, q_ref, k_ref, u_sc, *, seq_tiles):
    tm = x_ref.shape[0]
    xb = _layer_norm(x_ref[...], lng_ref[...], lnb_ref[...]).astype(BF16)
    _to_chunk_major(_dot(xb, wu_ref[...]), u_sc, u2_ref)
    zs_ref[...] = _dot(xb, wzs_ref[...]).astype(BF16)
    za_ref[...] = _dot(xb, wza_ref[...]).astype(BF16)
    gs_ref[...] = _dot(xb, wgs_ref[...]).astype(BF16)
    ga_ref[...] = _dot(xb, wga_ref[...]).astype(BF16)

    pos0 = (pl.program_id(0) % seq_tiles) * tm
    pos = (pos0 + lax.broadcasted_iota(jnp.int32, (tm, LANES), 0)).astype(F32)
    ang = pos * inv_ref[...]
    cs = jnp.cos(ang)
    sn = jnp.sin(ang) * sgn_ref[...]

    def rope(a, b):
        return a * cs + b * sn

    cqn = _rms_norm(_dot(xb, wcq_ref[...]), qng_ref[...]).astype(BF16)
    qf = _dot(cqn, wq_ref[...])
    for h in range(N_HEADS):
        base = 3 * LANES * h
        lat = qf[:, base:base + LANES]
        rot = rope(qf[:, base + LANES:base + 2 * LANES], qf[:, base + 2 * LANES:base + 3 * LANES])
        q_ref[0, h] = jnp.concatenate([lat, rot], axis=1).astype(BF16)

    kvn = _rms_norm(_dot(xb, wckv_ref[...]), kvng_ref[...])
    kr = _dot(xb, wkr_ref[...])
    krot = rope(kr[:, :LANES], kr[:, LANES:])
    k_ref[0] = jnp.concatenate([kvn, krot], axis=1).astype(BF16)


def _pre_call(x2d, bsz, seq, w):
    tokens = x2d.shape[0]
    tm = min(PRE_TILE, seq)
    seq_tiles = seq // tm
    n_tiles = tokens // tm
    row = lambda i: (i, 0)
    const = lambda i: (0, 0)
    full = lambda a: pl.BlockSpec(a.shape, const)
    weights = [w['ln_in_g'], w['ln_in_b'], w['w_u'], w['w_zs'], w['w_za'], w['w_gs'], w['w_ga'],
               w['w_cq'], w['w_ckv'], w['w_kr'], w['q_norm_g'], w['w_q'], w['kv_norm_g'],
               w['rope_inv'], w['rope_sgn']]
    out_shape = (
        jax.ShapeDtypeStruct((SSM_G // 2, tokens // SSM_CHUNK, SSM_PAIR), BF16),
        jax.ShapeDtypeStruct((tokens, SSM_WIDTH), BF16),
        jax.ShapeDtypeStruct((tokens, ATTN_WIDTH), BF16),
        jax.ShapeDtypeStruct((tokens, D_MODEL), BF16),
        jax.ShapeDtypeStruct((tokens, D_MODEL), BF16),
        jax.ShapeDtypeStruct((bsz, N_HEADS, seq, QK_WIDTH), BF16),
        jax.ShapeDtypeStruct((bsz, seq, QK_WIDTH), BF16),
    )
    out_specs = (
        pl.BlockSpec((SSM_G // 2, tm // SSM_CHUNK, SSM_PAIR), lambda i: (0, i, 0)),
        pl.BlockSpec((tm, SSM_WIDTH), row),
        pl.BlockSpec((tm, ATTN_WIDTH), row),
        pl.BlockSpec((tm, D_MODEL), row),
        pl.BlockSpec((tm, D_MODEL), row),
        pl.BlockSpec((1, N_HEADS, tm, QK_WIDTH), lambda i: (i // seq_tiles, 0, i % seq_tiles, 0)),
        pl.BlockSpec((1, tm, QK_WIDTH), lambda i: (i // seq_tiles, i % seq_tiles, 0)),
    )
    return pl.pallas_call(
        functools.partial(_pre_kernel, seq_tiles=seq_tiles),
        grid=(n_tiles,),
        in_specs=[pl.BlockSpec((tm, D_MODEL), row)] + [full(a) for a in weights],
        out_specs=out_specs,
        out_shape=out_shape,
        scratch_shapes=[pltpu.VMEM((SSM_WIDTH // LANES, tm, LANES), F32)],
        compiler_params=pltpu.CompilerParams(
            dimension_semantics=("arbitrary",), vmem_limit_bytes=VMEM_LIMIT_BYTES),
        name="pre",
    )(x2d, *weights)


def _cmul_add(ar, ai, xr, xi, yr, yi):
    return ar * xr - ai * xi + yr, ar * xi + ai * xr + yi


def _tile_scan(sr, si, cr, ci, pw, reverse):
    row = lax.broadcasted_iota(jnp.int32, (SUBLANES, LANES), 0)
    xr, xi = sr, si
    for k, s in enumerate((1, 2, 4)):
        shift = SUBLANES - s if reverse else s
        keep = (row < SUBLANES - s) if reverse else (row >= s)
        shr = jnp.where(keep, pltpu.roll(xr, shift, 0), 0.0)
        shi = jnp.where(keep, pltpu.roll(xi, shift, 0), 0.0)
        ar, ai = pw(SUBLANES + k)
        xr, xi = _cmul_add(ar, ai, shr, shi, xr, xi)
    pr, pi = pw(slice(0, SUBLANES))
    hr, hi = _cmul_add(pr, pi, cr, ci, xr, xi)
    edge = 0 if reverse else SUBLANES - 1
    out_r = jnp.broadcast_to(hr[edge:edge + 1], (SUBLANES, LANES))
    out_i = jnp.broadcast_to(hi[edge:edge + 1], (SUBLANES, LANES))
    first = (row == SUBLANES - 1) if reverse else (row == 0)
    one = SUBLANES - 1 if reverse else 1
    in_r = jnp.where(first, cr, pltpu.roll(hr, one, 0))
    in_i = jnp.where(first, ci, pltpu.roll(hi, one, 0))
    return in_r, in_i, out_r, out_i


def _s5_kernel(u_ref, t_ref, win_ref, wout_ref, pw_ref, y_ref, s_sc, h_sc, *, nb, nchunk):
    u = u_ref[0]
    s_sc[...] = _dot(u, win_ref[0])
    ntile = nchunk // SUBLANES

    def table(direction):
        base = direction * 2 * S5_PW_ROWS
        def pw(k):
            k = slice(k, k + 1) if isinstance(k, int) else k
            re = pw_ref[0, base + k.start:base + k.stop, :]
            im = pw_ref[0, base + S5_PW_ROWS + k.start:base + S5_PW_ROWS + k.stop, :]
            return re, im
        return pw

    pw_f, pw_b = table(0), table(1)

    def step(t, carry):
        new = []
        for b in range(nb):
            cfr, cfi, cbr, cbi = carry[b]
            rf = pl.multiple_of(b * nchunk + t * SUBLANES, SUBLANES)
            rb = pl.multiple_of(b * nchunk + (ntile - 1 - t) * SUBLANES, SUBLANES)
            rows_f, rows_b = pl.ds(rf, SUBLANES), pl.ds(rb, SUBLANES)
            in_r, in_i, cfr, cfi = _tile_scan(s_sc[rows_f, 0:LANES], s_sc[rows_f, LANES:2 * LANES],
                                              cfr, cfi, pw_f, False)
            h_sc[rows_f, 0:LANES] = in_r
            h_sc[rows_f, LANES:2 * LANES] = in_i
            in_r, in_i, cbr, cbi = _tile_scan(s_sc[rows_b, 2 * LANES:3 * LANES], s_sc[rows_b, 3 * LANES:4 * LANES],
                                              cbr, cbi, pw_b, True)
            h_sc[rows_b, 2 * LANES:3 * LANES] = in_r
            h_sc[rows_b, 3 * LANES:4 * LANES] = in_i
            new.append((cfr, cfi, cbr, cbi))
        return tuple(new)

    zero = jnp.zeros((SUBLANES, LANES), F32)
    lax.fori_loop(0, ntile, step, tuple((zero, zero, zero, zero) for _ in range(nb)))
    y_ref[0] = _dot(u, t_ref[0]) + _dot(h_sc[...].astype(BF16), wout_ref[0])


def _s5_call(u2, nb, nchunk, w):
    npair, rows, width = u2.shape
    blk = lambda i: (i, 0, 0)
    return pl.pallas_call(
        functools.partial(_s5_kernel, nb=nb, nchunk=nchunk),
        grid=(npair,),
        in_specs=[pl.BlockSpec((1, rows, width), blk),
                  pl.BlockSpec((1, width, width), blk),
                  pl.BlockSpec((1, width, width), blk),
                  pl.BlockSpec((1, width, width), blk),
                  pl.BlockSpec((1, 4 * S5_PW_ROWS, LANES), blk)],
        out_specs=pl.BlockSpec((1, rows, width), blk),
        out_shape=jax.ShapeDtypeStruct((npair, rows, width), F32),
        scratch_shapes=[pltpu.VMEM((rows, width), F32), pltpu.VMEM((rows, width), F32)],
        compiler_params=pltpu.CompilerParams(
            dimension_semantics=("arbitrary",), vmem_limit_bytes=VMEM_LIMIT_BYTES),
        name="s5",
    )(u2, w['s5_t'], w['s5_win'], w['s5_wout'], w['s5_pw'])


def _attn_kernel(q_ref, k_ref, wuv_ref, o_ref, m_sc, l_sc, acc_sc, s_sc, *, tk):
    nh, tq, width = q_ref.shape[1:]
    rows = nh * tq
    nk = k_ref.shape[1] // tk
    q = q_ref[0].reshape(rows, width)
    m_sc[...] = jnp.full(m_sc.shape, -1e30, F32)
    l_sc[...] = jnp.zeros(l_sc.shape, F32)
    acc_sc[...] = jnp.zeros(acc_sc.shape, F32)

    def key_rows(kt):
        return pl.ds(pl.multiple_of(kt * tk, tk), tk)

    def scores(kt):
        kblk = k_ref[0, key_rows(kt), :]
        return lax.dot_general(q, kblk, (((1,), (1,)), ((), ())), preferred_element_type=F32)

    def absorb(slot, kt):
        s = s_sc[slot]
        m_prev = m_sc[...]
        m_next = jnp.maximum(m_prev, jnp.max(s, axis=1, keepdims=True))
        alpha = jnp.exp2(m_prev - m_next)
        p = jnp.exp2(s - jnp.tile(m_next, (1, tk // LANES)))
        psum = p[:, 0:LANES]
        for c in range(1, tk // LANES):
            psum = psum + p[:, c * LANES:(c + 1) * LANES]
        l_sc[...] = alpha * l_sc[...] + psum
        acc_sc[...] = alpha * acc_sc[...] + _dot(p.astype(BF16), k_ref[0, key_rows(kt), 0:LANES])
        m_sc[...] = m_next

    s_sc[0] = scores(0)

    def body(i, carry):
        kt = 2 * i
        s_sc[1] = scores(kt + 1)
        absorb(0, kt)
        s_sc[0] = scores(kt + 2)
        absorb(1, kt + 1)
        return carry

    lax.fori_loop(0, nk // 2 - 1, body, 0)
    s_sc[1] = scores(nk - 1)
    absorb(0, nk - 2)
    absorb(1, nk - 1)
    l_row = jnp.sum(l_sc[...], axis=1, keepdims=True)
    lat = (acc_sc[...] / l_row).astype(BF16)
    outs = [_dot(lat[h * tq:(h + 1) * tq], wuv_ref[h]) for h in range(nh)]
    o_ref[0] = jnp.concatenate(outs, axis=1)


def _attn_call(q, k, wuv):
    bsz, nh, seq, width = q.shape
    tq = min(ATTN_TQ, seq)
    tk = min(ATTN_TK, seq // 2)
    assert seq % (2 * tk) == 0 and seq % tq == 0
    rows = nh * tq
    return pl.pallas_call(
        functools.partial(_attn_kernel, tk=tk),
        grid=(bsz, seq // tq),
        in_specs=[pl.BlockSpec((1, nh, tq, width), lambda b, i: (b, 0, i, 0)),
                  pl.BlockSpec((1, seq, width), lambda b, i: (b, 0, 0)),
                  pl.BlockSpec(wuv.shape, lambda b, i: (0, 0, 0))],
        out_specs=pl.BlockSpec((1, tq, ATTN_WIDTH), lambda b, i: (b, i, 0)),
        out_shape=jax.ShapeDtypeStruct((bsz, seq, ATTN_WIDTH), F32),
        scratch_shapes=[pltpu.VMEM((rows, LANES), F32), pltpu.VMEM((rows, LANES), F32),
                        pltpu.VMEM((rows, LANES), F32), pltpu.VMEM((2, rows, tk), F32)],
        compiler_params=pltpu.CompilerParams(
            dimension_semantics=("arbitrary", "arbitrary"), vmem_limit_bytes=VMEM_LIMIT_BYTES),
        name="attn",
    )(q, k, wuv)


def _post_kernel(x_ref, y2_ref, zs_ref, at_ref, za_ref, gs_ref, ga_ref, lig_ref, lib_ref,
                 wglu_ref, bglu_ref, wbs_ref, wba_ref, wo_ref, lng_ref, lnb_ref, o_ref, y_sc):
    xln = _layer_norm(x_ref[...], lig_ref[...], lib_ref[...])
    ys = jax.nn.gelu(_to_token_major(y2_ref, y_sc))
    ys = ys * _sigmoid(_dot(ys.astype(BF16), wglu_ref[...]) + bglu_ref[...])
    zs = zs_ref[...].astype(F32)
    ys = ys * (zs * _sigmoid(zs))
    za = za_ref[...].astype(F32)
    ya = at_ref[...] * (za * _sigmoid(za))
    merged = (_sigmoid(gs_ref[...].astype(F32)) * _dot(ys.astype(BF16), wbs_ref[...])
              + _sigmoid(ga_ref[...].astype(F32)) * _dot(ya.astype(BF16), wba_ref[...]))
    r = ALPHA * xln + _dot(merged.astype(BF16), wo_ref[...])
    o_ref[...] = _layer_norm(r, lng_ref[...], lnb_ref[...])


def _post_call(x2d, y2, zs, at, za, gs, ga, w):
    tokens = x2d.shape[0]
    tm = min(PRE_TILE, tokens)
    row = lambda i: (i, 0)
    const = lambda i: (0, 0)
    full = lambda a: pl.BlockSpec(a.shape, const)
    acts = [x2d, zs, at, za, gs, ga]
    weights = [w['ln_in_g'], w['ln_in_b'], w['w_glu'], w['b_glu'], w['w_bs'], w['w_ba'], w['w_o'],
               w['ln_g'], w['ln_b']]
    return pl.pallas_call(
        _post_kernel,
        grid=(tokens // tm,),
        in_specs=([pl.BlockSpec((tm, D_MODEL), row),
                   pl.BlockSpec((SSM_G // 2, tm // SSM_CHUNK, SSM_PAIR), lambda i: (0, i, 0))]
                  + [pl.BlockSpec((tm, a.shape[1]), row) for a in acts[1:]]
                  + [full(a) for a in weights]),
        out_specs=pl.BlockSpec((tm, D_MODEL), row),
        out_shape=jax.ShapeDtypeStruct((tokens, D_MODEL), F32),
        scratch_shapes=[pltpu.VMEM((SSM_WIDTH // LANES, tm, LANES), F32)],
        compiler_params=pltpu.CompilerParams(
            dimension_semantics=("arbitrary",), vmem_limit_bytes=VMEM_LIMIT_BYTES),
        name="post",
    )(x2d, y2, *acts[1:], *weights)


def _s5_operators(p):
    q = SSM_CHUNK
    g, n, pch = SSM_G, SSM_N, SSM_P
    k = jnp.arange(q + 1, dtype=F32)

    def direction(a_re, a_im, log_dt, c_re, c_im):
        dt = jnp.exp(log_dt)[:, None]
        er = jnp.exp(a_re * dt)
        lbr, lbi = er * jnp.cos(a_im * dt), er * jnp.sin(a_im * dt)
        den = a_re * a_re + a_im * a_im
        nr, ni = lbr - 1.0, lbi
        cfr = (nr * a_re + ni * a_im) / den
        cfi = (ni * a_re - nr * a_im) / den
        btr = p['b_re'] * cfr[..., None] - p['b_im'] * cfi[..., None]
        bti = p['b_re'] * cfi[..., None] + p['b_im'] * cfr[..., None]
        mag = jnp.exp(k[:, None, None] * (a_re * dt)[None])
        ph = k[:, None, None] * (a_im * dt)[None]
        pwr, pwi = mag * jnp.cos(ph), mag * jnp.sin(ph)
        cpr = c_re[None] * pwr[:, :, None, :] - c_im[None] * pwi[:, :, None, :]
        cpi = c_re[None] * pwi[:, :, None, :] + c_im[None] * pwr[:, :, None, :]
        taps = (jnp.einsum('tgpn,gnq->tgpq', cpr, btr, precision=HIGHEST)
                - jnp.einsum('tgpn,gnq->tgpq', cpi, bti, precision=HIGHEST))
        pbr = pwr[..., None] * btr[None] - pwi[..., None] * bti[None]
        pbi = pwr[..., None] * bti[None] + pwi[..., None] * btr[None]
        return taps, (pbr, pbi), (cpr, cpi), (pwr[q], pwi[q])

    tf, pbf, cpf, af = direction(p['a_re_f'], p['a_im_f'], p['log_dt_f'], p['c_re_f'], p['c_im_f'])
    tb, pbb, cpb, ab = direction(p['a_re_b'], p['a_im_b'], p['log_dt_b'], p['c_re_b'], p['c_im_b'])

    gp = g // 2
    w2 = 2 * q * pch
    same = jnp.eye(2, dtype=F32)

    ii = jnp.arange(q)[None, :]
    jj = jnp.arange(q)[:, None]
    lag_f = jnp.clip(ii - jj, 0, q)
    lag_b = jnp.clip(jj - ii, 0, q)
    kf = tf[lag_f] * (jj <= ii)[..., None, None, None]
    kb = tb[lag_b] * (jj >= ii)[..., None, None, None]
    dsk = p['d_skip'].reshape(g, pch)
    skip = (jj == ii)[..., None, None, None] * (jnp.eye(pch, dtype=F32)[None, None, None] * dsk[None, None, :, :, None])
    t6 = jnp.transpose((kf + kb + skip).reshape(q, q, gp, 2, pch, pch), (2, 0, 3, 5, 1, 4))
    t2 = t6[:, :, :, :, :, None, :] * same[None, None, :, None, None, :, None]

    idx_f = q - 1 - jnp.arange(q)
    idx_b = jnp.arange(q)
    win = jnp.stack([pbf[0][idx_f], pbf[1][idx_f], pbb[0][idx_b], pbb[1][idx_b]])
    win = jnp.transpose(win.reshape(4, q, gp, 2, n, pch), (2, 1, 3, 5, 0, 4))
    win2 = win[:, :, :, :, :, None, :] * same[None, None, :, None, None, :, None]

    odx_f = jnp.arange(q) + 1
    odx_b = q - jnp.arange(q)
    wout = jnp.stack([cpf[0][odx_f], -cpf[1][odx_f], cpb[0][odx_b], -cpb[1][odx_b]])
    wout = jnp.transpose(wout.reshape(4, q, gp, 2, pch, n), (2, 0, 3, 5, 1, 4))
    wout2 = wout[:, :, :, :, :, None, :] * same[None, None, :, None, None, :, None]

    def powers(a, reverse):
        ar, ai = a
        pows = [(ar, ai)]
        for _ in range(SUBLANES - 1):
            pr, pi = pows[-1]
            pows.append((pr * ar - pi * ai, pr * ai + pi * ar))
        per_row = pows[::-1] if reverse else pows
        rows = per_row + [pows[0], pows[1], pows[3]]
        rows = rows + [(jnp.zeros_like(ar), jnp.zeros_like(ar))] * (S5_PW_ROWS - len(rows))
        re = jnp.stack([r[0] for r in rows], axis=1)
        im = jnp.stack([r[1] for r in rows], axis=1)
        both = jnp.stack([re, im], axis=1)
        both = both.reshape(gp, 2, 2, S5_PW_ROWS, n)
        return jnp.transpose(both, (0, 2, 3, 1, 4)).reshape(gp, 2 * S5_PW_ROWS, 2 * n)

    pw2 = jnp.concatenate([powers(af, False), powers(ab, True)], axis=1)
    return (t2.reshape(gp, w2, w2).astype(BF16),
            win2.reshape(gp, w2, 8 * n).astype(BF16),
            wout2.reshape(gp, 8 * n, w2).astype(BF16),
            pw2)


def _prep_weights(ln_in_g, ln_in_b, w_in, ssm_b_re, ssm_b_im, ssm_a_re_fwd, ssm_a_im_fwd,
                  ssm_log_dt_fwd, ssm_c_re_fwd, ssm_c_im_fwd, ssm_a_re_bwd, ssm_a_im_bwd,
                  ssm_log_dt_bwd, ssm_c_re_bwd, ssm_c_im_bwd, ssm_d, w_glu, b_glu, q_norm_g, w_uq,
                  kv_norm_g, w_ukv, w_branch_ssm, w_branch_attn, w_o, ln_g, ln_b):
    l = 0
    w = {}
    row = lambda v: v.reshape(1, -1).astype(F32)
    w['ln_in_g'], w['ln_in_b'] = row(ln_in_g), row(ln_in_b)
    splits = [0]
    for s in IN_SIZES:
        splits.append(splits[-1] + s)
    seg = lambda i: w_in[l][:, splits[i]:splits[i + 1]]
    w['w_u'], w['w_zs'] = seg(0).astype(BF16), seg(1).astype(BF16)
    w['w_cq'], w['w_ckv'] = seg(2).astype(BF16), seg(3).astype(BF16)
    kr = seg(4)
    pad = jnp.zeros((D_MODEL, LANES - QK_ROPE), F32)
    k1, k2 = kr[:, :HALF_ROPE], kr[:, HALF_ROPE:]
    w['w_kr'] = jnp.concatenate([k1, k2, pad, k2, k1, pad], axis=1).astype(BF16)
    w['w_za'], w['w_gs'], w['w_ga'] = seg(5).astype(BF16), seg(6).astype(BF16), seg(7).astype(BF16)

    wuq = w_uq[l].reshape(Q_LORA, N_HEADS, QK_NOPE + QK_ROPE)
    wukv = w_ukv[l].reshape(KV_LORA, N_HEADS, QK_NOPE + V_HEAD)
    w_lat = jnp.einsum('chd,khd->chk', wuq[:, :, :QK_NOPE], wukv[:, :, :QK_NOPE], precision=HIGHEST)
    r1 = wuq[:, :, QK_NOPE:QK_NOPE + HALF_ROPE]
    r2 = wuq[:, :, QK_NOPE + HALF_ROPE:]
    zpad = jnp.zeros((Q_LORA, N_HEADS, LANES - QK_ROPE), F32)
    qscale = (QK_NOPE + QK_ROPE) ** -0.5 * math.log2(math.e)
    wq = jnp.concatenate([w_lat, r1, r2, zpad, r2, r1, zpad], axis=2) * qscale
    w['w_q'] = wq.reshape(Q_LORA, N_HEADS * 3 * LANES).astype(BF16)
    w['w_uv'] = jnp.transpose(wukv[:, :, QK_NOPE:], (1, 0, 2)).astype(BF16)
    w['q_norm_g'], w['kv_norm_g'] = row(q_norm_g[l]), row(kv_norm_g[l])

    inv = ROPE_THETA ** (-jnp.arange(HALF_ROPE, dtype=F32) * 2.0 / QK_ROPE)
    zl = jnp.zeros((LANES - QK_ROPE,), F32)
    w['rope_inv'] = jnp.concatenate([inv, inv, zl]).reshape(1, LANES)
    one = jnp.ones((HALF_ROPE,), F32)
    w['rope_sgn'] = jnp.concatenate([-one, one, zl]).reshape(1, LANES)

    p = dict(b_re=ssm_b_re[l], b_im=ssm_b_im[l], d_skip=ssm_d[l],
             a_re_f=ssm_a_re_fwd[l], a_im_f=ssm_a_im_fwd[l], log_dt_f=ssm_log_dt_fwd[l],
             c_re_f=ssm_c_re_fwd[l], c_im_f=ssm_c_im_fwd[l],
             a_re_b=ssm_a_re_bwd[l], a_im_b=ssm_a_im_bwd[l], log_dt_b=ssm_log_dt_bwd[l],
             c_re_b=ssm_c_re_bwd[l], c_im_b=ssm_c_im_bwd[l])
    w['s5_t'], w['s5_win'], w['s5_wout'], w['s5_pw'] = _s5_operators(p)

    w['w_glu'], w['b_glu'] = w_glu[l].astype(BF16), row(b_glu[l])
    w['w_bs'], w['w_ba'] = w_branch_ssm[l].astype(BF16), w_branch_attn[l].astype(BF16)
    w['w_o'] = w_o[l].astype(BF16)
    w['ln_g'], w['ln_b'] = row(ln_g[l]), row(ln_b[l])
    return w


def _trunk(x, w):
    bsz, seq, _ = x.shape
    tokens = bsz * seq
    x2d = x.reshape(tokens, D_MODEL)
    u2, zs, za, gs, ga, q, k = _pre_call(x2d, bsz, seq, w)
    y2 = _s5_call(u2, bsz, seq // SSM_CHUNK, w)
    at = _attn_call(q, k, w['w_uv']).reshape(tokens, ATTN_WIDTH)
    out = _post_call(x2d, y2, zs, at, za, gs, ga, w)
    return out.reshape(bsz, seq, D_MODEL)


def kernel(x_prompt, x_sample, ln_in_g, ln_in_b, w_in, ssm_b_re, ssm_b_im, ssm_a_re_fwd, ssm_a_im_fwd, ssm_log_dt_fwd, ssm_c_re_fwd, ssm_c_im_fwd, ssm_a_re_bwd, ssm_a_im_bwd, ssm_log_dt_bwd, ssm_c_re_bwd, ssm_c_im_bwd, ssm_d, w_glu, b_glu, q_norm_g, w_uq, kv_norm_g, w_ukv, w_branch_ssm, w_branch_attn, w_o, ln_g, ln_b):
    w = _prep_weights(ln_in_g, ln_in_b, w_in, ssm_b_re, ssm_b_im, ssm_a_re_fwd, ssm_a_im_fwd,
                      ssm_log_dt_fwd, ssm_c_re_fwd, ssm_c_im_fwd, ssm_a_re_bwd, ssm_a_im_bwd,
                      ssm_log_dt_bwd, ssm_c_re_bwd, ssm_c_im_bwd, ssm_d, w_glu, b_glu, q_norm_g,
                      w_uq, kv_norm_g, w_ukv, w_branch_ssm, w_branch_attn, w_o, ln_g, ln_b)
    return (_trunk(x_prompt, w), _trunk(x_sample, w))
```

```python
import functools
import math

import jax
import jax.numpy as jnp
from jax import lax
from jax.experimental import pallas as pl
from jax.experimental.pallas import tpu as pltpu

F32 = jnp.float32
BF16 = jnp.bfloat16
HIGHEST = lax.Precision.HIGHEST

D_MODEL = 1024
DEPTH = 1
SSM_WIDTH = 512
SSM_P = 16
SSM_G = 32
SSM_N = 64
N_HEADS = 8
QK_NOPE = 64
QK_ROPE = 32
V_HEAD = 64
Q_LORA = 256
KV_LORA = 128
ATTN_WIDTH = N_HEADS * V_HEAD
ROPE_THETA = 10000.0
LN_EPS = 1e-5
RMS_EPS = 1e-6
ALPHA = (2.0 * DEPTH) ** 0.25
IN_SIZES = (SSM_WIDTH, SSM_WIDTH, Q_LORA, KV_LORA, QK_ROPE, ATTN_WIDTH, D_MODEL, D_MODEL)

LANES = 128
SUBLANES = 8
S5_PW_ROWS = 16
VMEM_LIMIT_BYTES = 48 * 1024 * 1024

SSM_CHUNK = 16
SSM_PAIR = 2 * SSM_CHUNK * SSM_P
PAIR_LANES = 2 * SSM_P
PAIRS_PER_VREG = LANES // PAIR_LANES
PRE_TILE = 512
ATTN_TQ = 128
ATTN_TK = 1024
ATTN_TILES_PER_TRIP = 4
ATTN_OFFSET_SLACK = 64.0
QK_WIDTH = 2 * LANES
HALF_ROPE = QK_ROPE // 2


def _dot(a, b):
    return jnp.dot(a, b, preferred_element_type=F32)


def _layer_norm(x, g, b):
    mu = jnp.mean(x, axis=-1, keepdims=True)
    xc = x - mu
    var = jnp.mean(xc * xc, axis=-1, keepdims=True)
    return xc * lax.rsqrt(var + LN_EPS) * g + b


def _rms_norm(x, g):
    return x * lax.rsqrt(jnp.mean(x * x, axis=-1, keepdims=True) + RMS_EPS) * g


def _sigmoid(x):
    return 1.0 / (1.0 + jnp.exp(-x))


def _pair_slot(rows):
    return lax.broadcasted_iota(jnp.int32, (rows, LANES), 1) // PAIR_LANES


def _merge_slots(pieces, shifts, slot):
    out = None
    for k, (x, sh) in enumerate(zip(pieces, shifts)):
        x = pltpu.roll(x, sh % LANES, 1) if sh % LANES else x
        out = x if out is None else jnp.where(slot == k, x, out)
    return out


def _to_chunk_major(u, u_sc, u2_ref):
    rows = u.shape[0] // SSM_CHUNK
    slot = _pair_slot(rows)
    n = PAIRS_PER_VREG
    for b in range(SSM_WIDTH // LANES):
        u_sc[b] = u[:, b * LANES:(b + 1) * LANES]
        for v in range(SSM_PAIR // LANES):
            xs = [u_sc[b, pl.ds(n * v + s, rows, stride=SSM_CHUNK), :] for s in range(n)]
            for q in range(n):
                blk = _merge_slots(xs, [(s - q) * PAIR_LANES for s in range(n)], slot)
                u2_ref[n * b + q, :, v * LANES:(v + 1) * LANES] = blk.astype(u2_ref.dtype)


def _to_token_major(y2_ref, y_sc):
    rows = y2_ref.shape[1]
    slot = _pair_slot(rows)
    n = PAIRS_PER_VREG
    for b in range(SSM_WIDTH // LANES):
        for v in range(SSM_PAIR // LANES):
            ys = [y2_ref[n * b + q, :, v * LANES:(v + 1) * LANES] for q in range(n)]
            for s in range(n):
                blk = _merge_slots(ys, [(q - s) * PAIR_LANES for q in range(n)], slot)
                y_sc[b, pl.ds(n * v + s, rows, stride=SSM_CHUNK), :] = blk
    return jnp.concatenate([y_sc[b] for b in range(SSM_WIDTH // LANES)], axis=1)


def _pre_kernel(x_ref, lng_ref, lnb_ref, wu_ref, wzs_ref, wza_ref, wgs_ref, wga_ref,
                wcq_ref, wckv_ref, wkr_ref, qng_ref, wq_ref, kvng_ref, inv_ref, sgn_ref,
                u2_ref, zs_ref, za_ref, gs_ref, ga_ref, q_ref, k_ref, u_sc, *, seq_tiles):
    tm = x_ref.shape[0]
    xb = _layer_norm(x_ref[...], lng_ref[...], lnb_ref[...]).astype(BF16)
    _to_chunk_major(_dot(xb, wu_ref[...]), u_sc, u2_ref)
    zs_ref[...] = _dot(xb, wzs_ref[...]).astype(BF16)
    za_ref[...] = _dot(xb, wza_ref[...]).astype(BF16)
    gs_ref[...] = _dot(xb, wgs_ref[...]).astype(BF16)
    ga_ref[...] = _dot(xb, wga_ref[...]).astype(BF16)

    pos0 = (pl.program_id(0) % seq_tiles) * tm
    pos = (pos0 + lax.broadcasted_iota(jnp.int32, (tm, LANES), 0)).astype(F32)
    ang = pos * inv_ref[...]
    cs = jnp.cos(ang)
    sn = jnp.sin(ang) * sgn_ref[...]

    def rope(a, b):
        return a * cs + b * sn

    cqn = _rms_norm(_dot(xb, wcq_ref[...]), qng_ref[...]).astype(BF16)
    qf = _dot(cqn, wq_ref[...])
    for h in range(N_HEADS):
        base = 3 * LANES * h
        lat = qf[:, base:base + LANES]
        rot = rope(qf[:, base + LANES:base + 2 * LANES], qf[:, base + 2 * LANES:base + 3 * LANES])
        q_ref[0, h] = jnp.concatenate([lat, rot], axis=1).astype(BF16)

    kvn = _rms_norm(_dot(xb, wckv_ref[...]), kvng_ref[...])
    kr = _dot(xb, wkr_ref[...])
    krot = rope(kr[:, :LANES], kr[:, LANES:])
    k_ref[0] = jnp.concatenate([kvn, krot], axis=1).astype(BF16)


def _pre_call(x2d, bsz, seq, w):
    tokens = x2d.shape[0]
    tm = min(PRE_TILE, seq)
    seq_tiles = seq // tm
    n_tiles = tokens // tm
    row = lambda i: (i, 0)
    const = lambda i: (0, 0)
    full = lambda a: pl.BlockSpec(a.shape, const)
    weights = [w['ln_in_g'], w['ln_in_b'], w['w_u'], w['w_zs'], w['w_za'], w['w_gs'], w['w_ga'],
               w['w_cq'], w['w_ckv'], w['w_kr'], w['q_norm_g'], w['w_q'], w['kv_norm_g'],
               w['rope_inv'], w['rope_sgn']]
    out_shape = (
        jax.ShapeDtypeStruct((SSM_G // 2, tokens // SSM_CHUNK, SSM_PAIR), BF16),
        jax.ShapeDtypeStruct((tokens, SSM_WIDTH), BF16),
        jax.ShapeDtypeStruct((tokens, ATTN_WIDTH), BF16),
        jax.ShapeDtypeStruct((tokens, D_MODEL), BF16),
        jax.ShapeDtypeStruct((tokens, D_MODEL), BF16),
        jax.ShapeDtypeStruct((bsz, N_HEADS, seq, QK_WIDTH), BF16),
        jax.ShapeDtypeStruct((bsz, seq, QK_WIDTH), BF16),
    )
    out_specs = (
        pl.BlockSpec((SSM_G // 2, tm // SSM_CHUNK, SSM_PAIR), lambda i: (0, i, 0)),
        pl.BlockSpec((tm, SSM_WIDTH), row),
        pl.BlockSpec((tm, ATTN_WIDTH), row),
        pl.BlockSpec((tm, D_MODEL), row),
        pl.BlockSpec((tm, D_MODEL), row),
        pl.BlockSpec((1, N_HEADS, tm, QK_WIDTH), lambda i: (i // seq_tiles, 0, i % seq_tiles, 0)),
        pl.BlockSpec((1, tm, QK_WIDTH), lambda i: (i // seq_tiles, i % seq_tiles, 0)),
    )
    return pl.pallas_call(
        functools.partial(_pre_kernel, seq_tiles=seq_tiles),
        grid=(n_tiles,),
        in_specs=[pl.BlockSpec((tm, D_MODEL), row)] + [full(a) for a in weights],
        out_specs=out_specs,
        out_shape=out_shape,
        scratch_shapes=[pltpu.VMEM((SSM_WIDTH // LANES, tm, LANES), F32)],
        compiler_params=pltpu.CompilerParams(
            dimension_semantics=("arbitrary",), vmem_limit_bytes=VMEM_LIMIT_BYTES),
        name="pre",
    )(x2d, *weights)


def _cmul_add(ar, ai, xr, xi, yr, yi):
    return ar * xr - ai * xi + yr, ar * xi + ai * xr + yi


def _tile_scan(sr, si, cr, ci, pw, reverse):
    row = lax.broadcasted_iota(jnp.int32, (SUBLANES, LANES), 0)
    xr, xi = sr, si
    for k, s in enumerate((1, 2, 4)):
        shift = SUBLANES - s if reverse else s
        keep = (row < SUBLANES - s) if reverse else (row >= s)
        shr = jnp.where(keep, pltpu.roll(xr, shift, 0), 0.0)
        shi = jnp.where(keep, pltpu.roll(xi, shift, 0), 0.0)
        ar, ai = pw(SUBLANES + k)
        xr, xi = _cmul_add(ar, ai, shr, shi, xr, xi)
    pr, pi = pw(slice(0, SUBLANES))
    hr, hi = _cmul_add(pr, pi, cr, ci, xr, xi)
    edge = 0 if reverse else SUBLANES - 1
    out_r = jnp.broadcast_to(hr[edge:edge + 1], (SUBLANES, LANES))
    out_i = jnp.broadcast_to(hi[edge:edge + 1], (SUBLANES, LANES))
    first = (row == SUBLANES - 1) if reverse else (row == 0)
    one = SUBLANES - 1 if reverse else 1
    in_r = jnp.where(first, cr, pltpu.roll(hr, one, 0))
    in_i = jnp.where(first, ci, pltpu.roll(hi, one, 0))
    return in_r, in_i, out_r, out_i


def _s5_kernel(u_ref, t_ref, win_ref, wout_ref, pw_ref, y_ref, s_sc, h_sc, *, nb, nchunk):
    u = u_ref[0]
    s_sc[...] = _dot(u, win_ref[0])
    ntile = nchunk // SUBLANES

    def table(direction):
        base = direction * 2 * S5_PW_ROWS
        def pw(k):
            k = slice(k, k + 1) if isinstance(k, int) else k
            re = pw_ref[0, base + k.start:base + k.stop, :]
            im = pw_ref[0, base + S5_PW_ROWS + k.start:base + S5_PW_ROWS + k.stop, :]
            return re, im
        return pw

    pw_f, pw_b = table(0), table(1)

    def step(t, carry):
        new = []
        for b in range(nb):
            cfr, cfi, cbr, cbi = carry[b]
            rf = pl.multiple_of(b * nchunk + t * SUBLANES, SUBLANES)
            rb = pl.multiple_of(b * nchunk + (ntile - 1 - t) * SUBLANES, SUBLANES)
            rows_f, rows_b = pl.ds(rf, SUBLANES), pl.ds(rb, SUBLANES)
            in_r, in_i, cfr, cfi = _tile_scan(s_sc[rows_f, 0:LANES], s_sc[rows_f, LANES:2 * LANES],
                                              cfr, cfi, pw_f, False)
            h_sc[rows_f, 0:LANES] = in_r
            h_sc[rows_f, LANES:2 * LANES] = in_i
            in_r, in_i, cbr, cbi = _tile_scan(s_sc[rows_b, 2 * LANES:3 * LANES], s_sc[rows_b, 3 * LANES:4 * LANES],
                                              cbr, cbi, pw_b, True)
            h_sc[rows_b, 2 * LANES:3 * LANES] = in_r
            h_sc[rows_b, 3 * LANES:4 * LANES] = in_i
            new.append((cfr, cfi, cbr, cbi))
        return tuple(new)

    zero = jnp.zeros((SUBLANES, LANES), F32)
    lax.fori_loop(0, ntile, step, tuple((zero, zero, zero, zero) for _ in range(nb)))
    y_ref[0] = _dot(u, t_ref[0]) + _dot(h_sc[...].astype(BF16), wout_ref[0])


def _s5_call(u2, nb, nchunk, w):
    npair, rows, width = u2.shape
    blk = lambda i: (i, 0, 0)
    return pl.pallas_call(
        functools.partial(_s5_kernel, nb=nb, nchunk=nchunk),
        grid=(npair,),
        in_specs=[pl.BlockSpec((1, rows, width), blk),
                  pl.BlockSpec((1, width, width), blk),
                  pl.BlockSpec((1, width, width), blk),
                  pl.BlockSpec((1, width, width), blk),
                  pl.BlockSpec((1, 4 * S5_PW_ROWS, LANES), blk)],
        out_specs=pl.BlockSpec((1, rows, width), blk),
        out_shape=jax.ShapeDtypeStruct((npair, rows, width), F32),
        scratch_shapes=[pltpu.VMEM((rows, width), F32), pltpu.VMEM((rows, width), F32)],
        compiler_params=pltpu.CompilerParams(
            dimension_semantics=("arbitrary",), vmem_limit_bytes=VMEM_LIMIT_BYTES),
        name="s5",
    )(u2, w['s5_t'], w['s5_win'], w['s5_wout'], w['s5_pw'])


def _lane_blocks(x, op):
    out = x[:, 0:LANES]
    for c in range(1, x.shape[1] // LANES):
        out = op(out, x[:, c * LANES:(c + 1) * LANES])
    return out


def _attn_kernel(q_ref, k_ref, wuv_ref, o_ref, m_sc, c_sc, l_sc, acc_sc, bad_sc, s_sc, *, tk):
    nh, tq, width = q_ref.shape[1:]
    rows = nh * tq
    nk = k_ref.shape[1] // tk

    def key_rows(kt):
        return pl.ds(pl.multiple_of(kt * tk, tk), tk)

    def scores(kt):
        q = q_ref[0].reshape(rows, width)
        return lax.dot_general(q, k_ref[0, key_rows(kt), :], (((1,), (1,)), ((), ())),
                               preferred_element_type=F32)

    def values(p, kt):
        return _dot(p.astype(BF16), k_ref[0, key_rows(kt), 0:LANES])

    def reset():
        m_sc[...] = jnp.full(m_sc.shape, -1e30, F32)
        l_sc[...] = jnp.zeros(l_sc.shape, F32)
        acc_sc[...] = jnp.zeros(acc_sc.shape, F32)

    reset()
    c_sc[...] = jnp.zeros(c_sc.shape, F32)
    bad_sc[...] = jnp.full(bad_sc.shape, -1.0, F32)

    group = min(ATTN_TILES_PER_TRIP, nk)

    def fast_group(i, carry):
        c = c_sc[...]
        c_wide = jnp.tile(c, (1, tk // LANES))
        l_new, acc_new, m_grp = l_sc[...], acc_sc[...], None
        for j in range(group):
            kt = group * i + j
            s = scores(kt)
            p = jnp.exp2(s - c_wide)
            blk_max = _lane_blocks(s, jnp.maximum)
            m_grp = blk_max if m_grp is None else jnp.maximum(m_grp, blk_max)
            l_new = l_new + _lane_blocks(p, jnp.add)
            acc_new = acc_new + values(p, kt)
        m_next = jnp.maximum(m_sc[...], jnp.max(m_grp, axis=1, keepdims=True))
        shift = m_next - c
        bad_sc[...] = jnp.maximum(bad_sc[...], jnp.abs(shift) - ATTN_OFFSET_SLACK)
        alpha = jnp.exp2(-shift)
        l_sc[...] = alpha * l_new
        acc_sc[...] = alpha * acc_new
        m_sc[...] = m_next
        c_sc[...] = m_next
        return carry

    lax.fori_loop(0, nk // group, fast_group, 0)

    def absorb(kt, carry):
        s_sc[...] = scores(kt)
        s = s_sc[...]
        m_prev = m_sc[...]
        m_next = jnp.maximum(m_prev, jnp.max(s, axis=1, keepdims=True))
        alpha = jnp.exp2(m_prev - m_next)
        p = jnp.exp2(s - jnp.tile(m_next, (1, tk // LANES)))
        l_sc[...] = alpha * l_sc[...] + _lane_blocks(p, jnp.add)
        acc_sc[...] = alpha * acc_sc[...] + values(p, kt)
        m_sc[...] = m_next
        return carry

    @pl.when(jnp.max(bad_sc[...]) > 0.0)
    def _():
        reset()
        lax.fori_loop(0, nk, absorb, 0)

    l_row = jnp.sum(l_sc[...], axis=1, keepdims=True)
    lat = (acc_sc[...] / l_row).astype(BF16)
    outs = [_dot(lat[h * tq:(h + 1) * tq], wuv_ref[h]) for h in range(nh)]
    o_ref[0] = jnp.concatenate(outs, axis=1)


def _attn_call(q, k, wuv):
    bsz, nh, seq, width = q.shape
    tq = min(ATTN_TQ, seq)
    tk = min(ATTN_TK, seq // 2)
    assert seq % tk == 0 and (seq // tk) % min(ATTN_TILES_PER_TRIP, seq // tk) == 0 and seq % tq == 0
    rows = nh * tq
    return pl.pallas_call(
        functools.partial(_attn_kernel, tk=tk),
        grid=(bsz, seq // tq),
        in_specs=[pl.BlockSpec((1, nh, tq, width), lambda b, i: (b, 0, i, 0)),
                  pl.BlockSpec((1, seq, width), lambda b, i: (b, 0, 0)),
                  pl.BlockSpec(wuv.shape, lambda b, i: (0, 0, 0))],
        out_specs=pl.BlockSpec((1, tq, ATTN_WIDTH), lambda b, i: (b, i, 0)),
        out_shape=jax.ShapeDtypeStruct((bsz, seq, ATTN_WIDTH), F32),
        scratch_shapes=[pltpu.VMEM((rows, LANES), F32) for _ in range(5)] + [pltpu.VMEM((rows, tk), F32)],
        compiler_params=pltpu.CompilerParams(
            dimension_semantics=("arbitrary", "arbitrary"), vmem_limit_bytes=VMEM_LIMIT_BYTES),
        name="attn",
    )(q, k, wuv)


def _post_kernel(x_ref, y2_ref, zs_ref, at_ref, za_ref, gs_ref, ga_ref, lig_ref, lib_ref,
                 wglu_ref, bglu_ref, wbs_ref, wba_ref, wo_ref, lng_ref, lnb_ref, o_ref, y_sc):
    xln = _layer_norm(x_ref[...], lig_ref[...], lib_ref[...])
    ys = jax.nn.gelu(_to_token_major(y2_ref, y_sc))
    ys = ys * _sigmoid(_dot(ys.astype(BF16), wglu_ref[...]) + bglu_ref[...])
    zs = zs_ref[...].astype(F32)
    ys = ys * (zs * _sigmoid(zs))
    za = za_ref[...].astype(F32)
    ya = at_ref[...] * (za * _sigmoid(za))
    merged = (_sigmoid(gs_ref[...].astype(F32)) * _dot(ys.astype(BF16), wbs_ref[...])
              + _sigmoid(ga_ref[...].astype(F32)) * _dot(ya.astype(BF16), wba_ref[...]))
    r = ALPHA * xln + _dot(merged.astype(BF16), wo_ref[...])
    o_ref[...] = _layer_norm(r, lng_ref[...], lnb_ref[...])


def _post_call(x2d, y2, zs, at, za, gs, ga, w):
    tokens = x2d.shape[0]
    tm = min(PRE_TILE, tokens)
    row = lambda i: (i, 0)
    const = lambda i: (0, 0)
    full = lambda a: pl.BlockSpec(a.shape, const)
    acts = [x2d, zs, at, za, gs, ga]
    weights = [w['ln_in_g'], w['ln_in_b'], w['w_glu'], w['b_glu'], w['w_bs'], w['w_ba'], w['w_o'],
               w['ln_g'], w['ln_b']]
    return pl.pallas_call(
        _post_kernel,
        grid=(tokens // tm,),
        in_specs=([pl.BlockSpec((tm, D_MODEL), row),
                   pl.BlockSpec((SSM_G // 2, tm // SSM_CHUNK, SSM_PAIR), lambda i: (0, i, 0))]
                  + [pl.BlockSpec((tm, a.shape[1]), row) for a in acts[1:]]
                  + [full(a) for a in weights]),
        out_specs=pl.BlockSpec((tm, D_MODEL), row),
        out_shape=jax.ShapeDtypeStruct((tokens, D_MODEL), F32),
        scratch_shapes=[pltpu.VMEM((SSM_WIDTH // LANES, tm, LANES), F32)],
        compiler_params=pltpu.CompilerParams(
            dimension_semantics=("arbitrary",), vmem_limit_bytes=VMEM_LIMIT_BYTES),
        name="post",
    )(x2d, y2, *acts[1:], *weights)


def _s5_operators(p):
    q = SSM_CHUNK
    gp, n2, c2 = SSM_G // 2, 2 * SSM_N, PAIR_LANES
    w2 = q * c2
    k = jnp.arange(q + 1, dtype=F32)
    same = jnp.eye(2, dtype=F32)

    def pair_states(v):
        return v.reshape(gp, n2)

    def direction(a_re, a_im, log_dt, c_re, c_im):
        a_re, a_im = pair_states(a_re), pair_states(a_im)
        dt = jnp.repeat(jnp.exp(log_dt), SSM_N).reshape(gp, n2)
        er = jnp.exp(a_re * dt)
        lbr, lbi = er * jnp.cos(a_im * dt), er * jnp.sin(a_im * dt)
        den = a_re * a_re + a_im * a_im
        nr, ni = lbr - 1.0, lbi
        cfr = (nr * a_re + ni * a_im) / den
        cfi = (ni * a_re - nr * a_im) / den
        def embed(x):
            x = x.reshape(gp, 2, SSM_P, SSM_N)
            x = x[:, :, :, None, :] * same[None, :, None, :, None]
            return x.reshape(gp, c2, n2)
        b_re = embed(jnp.swapaxes(p['b_re'], 1, 2))
        b_im = embed(jnp.swapaxes(p['b_im'], 1, 2))
        btr = b_re * cfr[:, None, :] - b_im * cfi[:, None, :]
        bti = b_re * cfi[:, None, :] + b_im * cfr[:, None, :]
        cr, ci = embed(c_re), embed(c_im)
        mag = jnp.exp(k[:, None, None] * (a_re * dt)[None])
        ph = k[:, None, None] * (a_im * dt)[None]
        pwr, pwi = mag * jnp.cos(ph), mag * jnp.sin(ph)
        cpr = cr[None] * pwr[:, :, None, :] - ci[None] * pwi[:, :, None, :]
        cpi = cr[None] * pwi[:, :, None, :] + ci[None] * pwr[:, :, None, :]
        pbr = btr[None] * pwr[:, :, None, :] - bti[None] * pwi[:, :, None, :]
        pbi = btr[None] * pwi[:, :, None, :] + bti[None] * pwr[:, :, None, :]
        taps = (jnp.einsum('gcn,tgdn->tgcd', btr, cpr, precision=HIGHEST)
                - jnp.einsum('gcn,tgdn->tgcd', bti, cpi, precision=HIGHEST))
        return taps, (pbr, pbi), (cpr, cpi), (pwr[q], pwi[q])

    tf, pbf, cpf, af = direction(p['a_re_f'], p['a_im_f'], p['log_dt_f'], p['c_re_f'], p['c_im_f'])
    tb, pbb, cpb, ab = direction(p['a_re_b'], p['a_im_b'], p['log_dt_b'], p['c_re_b'], p['c_im_b'])

    skip = jnp.eye(c2, dtype=F32)[None] * p['d_skip'].reshape(gp, 1, c2)
    centre = tf[0] + tb[0] + skip
    lags = jnp.concatenate([tb[q - 1:0:-1], centre[None], tf[1:q], jnp.zeros_like(tf[:1])])
    lags = jnp.transpose(lags, (1, 2, 0, 3)).reshape(gp, c2, 2 * w2).astype(BF16)
    t2 = jnp.stack([lags[:, :, (q - 1 - j) * c2:(q - 1 - j) * c2 + w2] for j in range(q)], axis=1)

    win = jnp.concatenate([pbf[0][q - 1::-1], pbf[1][q - 1::-1], pbb[0][:q], pbb[1][:q]], axis=-1)
    win2 = jnp.transpose(win.astype(BF16), (1, 0, 2, 3))

    wout = jnp.concatenate([cpf[0][1:], -cpf[1][1:], cpb[0][q:0:-1], -cpb[1][q:0:-1]], axis=-1)
    wout2 = jnp.swapaxes(jnp.transpose(wout.astype(BF16), (1, 0, 2, 3)).reshape(gp, w2, 4 * n2), 1, 2)

    def powers(a, reverse):
        ar, ai = a
        pows = [(ar, ai)]
        for _ in range(SUBLANES - 1):
            pr, pi = pows[-1]
            pows.append((pr * ar - pi * ai, pr * ai + pi * ar))
        per_row = pows[::-1] if reverse else pows
        rows = per_row + [pows[0], pows[1], pows[3]]
        rows = rows + [(jnp.zeros_like(ar), jnp.zeros_like(ar))] * (S5_PW_ROWS - len(rows))
        return jnp.concatenate([jnp.stack([r[0] for r in rows], axis=1),
                                jnp.stack([r[1] for r in rows], axis=1)], axis=1)

    pw2 = jnp.concatenate([powers(af, False), powers(ab, True)], axis=1)
    return t2.reshape(gp, w2, w2), win2.reshape(gp, w2, 4 * n2), wout2, pw2


def _prep_weights(ln_in_g, ln_in_b, w_in, ssm_b_re, ssm_b_im, ssm_a_re_fwd, ssm_a_im_fwd,
                  ssm_log_dt_fwd, ssm_c_re_fwd, ssm_c_im_fwd, ssm_a_re_bwd, ssm_a_im_bwd,
                  ssm_log_dt_bwd, ssm_c_re_bwd, ssm_c_im_bwd, ssm_d, w_glu, b_glu, q_norm_g, w_uq,
                  kv_norm_g, w_ukv, w_branch_ssm, w_branch_attn, w_o, ln_g, ln_b):
    l = 0
    w = {}
    row = lambda v: v.reshape(1, -1).astype(F32)
    w['ln_in_g'], w['ln_in_b'] = row(ln_in_g), row(ln_in_b)
    splits = [0]
    for s in IN_SIZES:
        splits.append(splits[-1] + s)
    seg = lambda i: w_in[l][:, splits[i]:splits[i + 1]]
    w['w_u'], w['w_zs'] = seg(0).astype(BF16), seg(1).astype(BF16)
    w['w_cq'], w['w_ckv'] = seg(2).astype(BF16), seg(3).astype(BF16)
    kr = seg(4)
    pad = jnp.zeros((D_MODEL, LANES - QK_ROPE), F32)
    k1, k2 = kr[:, :HALF_ROPE], kr[:, HALF_ROPE:]
    w['w_kr'] = jnp.concatenate([k1, k2, pad, k2, k1, pad], axis=1).astype(BF16)
    w['w_za'], w['w_gs'], w['w_ga'] = seg(5).astype(BF16), seg(6).astype(BF16), seg(7).astype(BF16)

    wuq = w_uq[l].reshape(Q_LORA, N_HEADS, QK_NOPE + QK_ROPE)
    wukv = w_ukv[l].reshape(KV_LORA, N_HEADS, QK_NOPE + V_HEAD)
    w_lat = jnp.einsum('chd,khd->chk', wuq[:, :, :QK_NOPE], wukv[:, :, :QK_NOPE], precision=HIGHEST)
    r1 = wuq[:, :, QK_NOPE:QK_NOPE + HALF_ROPE]
    r2 = wuq[:, :, QK_NOPE + HALF_ROPE:]
    zpad = jnp.zeros((Q_LORA, N_HEADS, LANES - QK_ROPE), F32)
    qscale = (QK_NOPE + QK_ROPE) ** -0.5 * math.log2(math.e)
    wq = jnp.concatenate([w_lat, r1, r2, zpad, r2, r1, zpad], axis=2) * qscale
    w['w_q'] = wq.reshape(Q_LORA, N_HEADS * 3 * LANES).astype(BF16)
    w['w_uv'] = jnp.transpose(wukv[:, :, QK_NOPE:], (1, 0, 2)).astype(BF16)
    w['q_norm_g'], w['kv_norm_g'] = row(q_norm_g[l]), row(kv_norm_g[l])

    inv = ROPE_THETA ** (-jnp.arange(HALF_ROPE, dtype=F32) * 2.0 / QK_ROPE)
    zl = jnp.zeros((LANES - QK_ROPE,), F32)
    w['rope_inv'] = jnp.concatenate([inv, inv, zl]).reshape(1, LANES)
    one = jnp.ones((HALF_ROPE,), F32)
    w['rope_sgn'] = jnp.concatenate([-one, one, zl]).reshape(1, LANES)

    p = dict(b_re=ssm_b_re[l], b_im=ssm_b_im[l], d_skip=ssm_d[l],
             a_re_f=ssm_a_re_fwd[l], a_im_f=ssm_a_im_fwd[l], log_dt_f=ssm_log_dt_fwd[l],
             c_re_f=ssm_c_re_fwd[l], c_im_f=ssm_c_im_fwd[l],
             a_re_b=ssm_a_re_bwd[l], a_im_b=ssm_a_im_bwd[l], log_dt_b=ssm_log_dt_bwd[l],
             c_re_b=ssm_c_re_bwd[l], c_im_b=ssm_c_im_bwd[l])
    w['s5_t'], w['s5_win'], w['s5_wout'], w['s5_pw'] = _s5_operators(p)

    w['w_glu'], w['b_glu'] = w_glu[l].astype(BF16), row(b_glu[l])
    w['w_bs'], w['w_ba'] = w_branch_ssm[l].astype(BF16), w_branch_attn[l].astype(BF16)
    w['w_o'] = w_o[l].astype(BF16)
    w['ln_g'], w['ln_b'] = row(ln_g[l]), row(ln_b[l])
    return w


def _trunk(x, w):
    bsz, seq, _ = x.shape
    tokens = bsz * seq
    x2d = x.reshape(tokens, D_MODEL)
    u2, zs, za, gs, ga, q, k = _pre_call(x2d, bsz, seq, w)
    y2 = _s5_call(u2, bsz, seq // SSM_CHUNK, w)
    at = _attn_call(q, k, w['w_uv']).reshape(tokens, ATTN_WIDTH)
    out = _post_call(x2d, y2, zs, at, za, gs, ga, w)
    return out.reshape(bsz, seq, D_MODEL)


def kernel(x_prompt, x_sample, ln_in_g, ln_in_b, w_in, ssm_b_re, ssm_b_im, ssm_a_re_fwd, ssm_a_im_fwd, ssm_log_dt_fwd, ssm_c_re_fwd, ssm_c_im_fwd, ssm_a_re_bwd, ssm_a_im_bwd, ssm_log_dt_bwd, ssm_c_re_bwd, ssm_c_im_bwd, ssm_d, w_glu, b_glu, q_norm_g, w_uq, kv_norm_g, w_ukv, w_branch_ssm, w_branch_attn, w_o, ln_g, ln_b):
    w = _prep_weights(ln_in_g, ln_in_b, w_in, ssm_b_re, ssm_b_im, ssm_a_re_fwd, ssm_a_im_fwd,
                      ssm_log_dt_fwd, ssm_c_re_fwd, ssm_c_im_fwd, ssm_a_re_bwd, ssm_a_im_bwd,
                      ssm_log_dt_bwd, ssm_c_re_bwd, ssm_c_im_bwd, ssm_d, w_glu, b_glu, q_norm_g,
                      w_uq, kv_norm_g, w_ukv, w_branch_ssm, w_branch_attn, w_o, ln_g, ln_b)
    return (_trunk(x_prompt, w), _trunk(x_sample, w))
```

```python
import functools
import math

import jax
import jax.numpy as jnp
from jax import lax
from jax.experimental import pallas as pl
from jax.experimental.pallas import tpu as pltpu

F32 = jnp.float32
BF16 = jnp.bfloat16
HIGHEST = lax.Precision.HIGHEST

D_MODEL = 1024
DEPTH = 1
SSM_WIDTH = 512
SSM_P = 16
SSM_G = 32
SSM_N = 64
N_HEADS = 8
QK_NOPE = 64
QK_ROPE = 32
V_HEAD = 64
Q_LORA = 256
KV_LORA = 128
ATTN_WIDTH = N_HEADS * V_HEAD
ROPE_THETA = 10000.0
LN_EPS = 1e-5
RMS_EPS = 1e-6
ALPHA = (2.0 * DEPTH) ** 0.25
IN_SIZES = (SSM_WIDTH, SSM_WIDTH, Q_LORA, KV_LORA, QK_ROPE, ATTN_WIDTH, D_MODEL, D_MODEL)

LANES = 128
SUBLANES = 8
S5_PW_ROWS = 16
VMEM_LIMIT_BYTES = 48 * 1024 * 1024

SSM_CHUNK = 16
SSM_PAIR = 2 * SSM_CHUNK * SSM_P
PAIR_LANES = 2 * SSM_P
PAIRS_PER_VREG = LANES // PAIR_LANES
PRE_TILE = 512
ATTN_TQ = 256
ATTN_TK = 1024
ATTN_TILES_PER_TRIP = 8
ATTN_OFFSET_SLACK = 64.0
QK_WIDTH = 2 * LANES
HALF_ROPE = QK_ROPE // 2


def _dot(a, b):
    return jnp.dot(a, b, preferred_element_type=F32)


def _layer_norm(x, g, b):
    mu = jnp.mean(x, axis=-1, keepdims=True)
    xc = x - mu
    var = jnp.mean(xc * xc, axis=-1, keepdims=True)
    return xc * lax.rsqrt(var + LN_EPS) * g + b


def _rms_norm(x, g):
    return x * lax.rsqrt(jnp.mean(x * x, axis=-1, keepdims=True) + RMS_EPS) * g


def _sigmoid(x):
    return 0.5 * jnp.tanh(0.5 * x) + 0.5


def _pair_slot(rows):
    return lax.broadcasted_iota(jnp.int32, (rows, LANES), 1) // PAIR_LANES


def _merge_slots(pieces, shifts, slot):
    out = None
    for k, (x, sh) in enumerate(zip(pieces, shifts)):
        x = pltpu.roll(x, sh % LANES, 1) if sh % LANES else x
        out = x if out is None else jnp.where(slot == k, x, out)
    return out


def _to_chunk_major(u, u_sc, u2_ref):
    rows = u.shape[0] // SSM_CHUNK
    slot = _pair_slot(rows)
    n = PAIRS_PER_VREG
    for b in range(SSM_WIDTH // LANES):
        u_sc[b] = u[:, b * LANES:(b + 1) * LANES]
        for v in range(SSM_PAIR // LANES):
            xs = [u_sc[b, pl.ds(n * v + s, rows, stride=SSM_CHUNK), :] for s in range(n)]
            for q in range(n):
                blk = _merge_slots(xs, [(s - q) * PAIR_LANES for s in range(n)], slot)
                u2_ref[n * b + q, :, v * LANES:(v + 1) * LANES] = blk.astype(u2_ref.dtype)


def _to_token_major(y2_ref, y_sc):
    rows = y2_ref.shape[1]
    slot = _pair_slot(rows)
    n = PAIRS_PER_VREG
    for b in range(SSM_WIDTH // LANES):
        for v in range(SSM_PAIR // LANES):
            ys = [y2_ref[n * b + q, :, v * LANES:(v + 1) * LANES] for q in range(n)]
            for s in range(n):
                blk = _merge_slots(ys, [(q - s) * PAIR_LANES for q in range(n)], slot)
                y_sc[b, pl.ds(n * v + s, rows, stride=SSM_CHUNK), :] = blk
    return jnp.concatenate([y_sc[b] for b in range(SSM_WIDTH // LANES)], axis=1)


def _pre_kernel(x_ref, lng_ref, lnb_ref, wu_ref, wzs_ref, wza_ref, wgs_ref, wga_ref,
                wcq_ref, wckv_ref, wkr_ref, qng_ref, wq_ref, kvng_ref, inv_ref, sgn_ref,
                u2_ref, zs_ref, za_ref, gs_ref, ga_ref, q_ref, k_ref, u_sc, *, seq_tiles):
    tm = x_ref.shape[0]
    xb = _layer_norm(x_ref[...], lng_ref[...], lnb_ref[...]).astype(BF16)

    cq = _dot(xb, wcq_ref[...])
    ckv = _dot(xb, wckv_ref[...])
    kr = _dot(xb, wkr_ref[...])

    _to_chunk_major(_dot(xb, wu_ref[...]), u_sc, u2_ref)
    zs_ref[...] = _dot(xb, wzs_ref[...]).astype(BF16)
    za_ref[...] = _dot(xb, wza_ref[...]).astype(BF16)
    gs_ref[...] = _dot(xb, wgs_ref[...]).astype(BF16)
    ga_ref[...] = _dot(xb, wga_ref[...]).astype(BF16)

    pos0 = (pl.program_id(0) % seq_tiles) * tm
    pos = (pos0 + lax.broadcasted_iota(jnp.int32, (tm, LANES), 0)).astype(F32)
    ang = pos * inv_ref[...]
    cs = jnp.cos(ang)
    sn = jnp.sin(ang) * sgn_ref[...]

    def rope(a, b):
        return a * cs + b * sn

    kvn = _rms_norm(ckv, kvng_ref[...])
    krot = rope(kr[:, :LANES], kr[:, LANES:])
    k_ref[0] = jnp.concatenate([kvn, krot], axis=1).astype(BF16)

    cqn = _rms_norm(cq, qng_ref[...]).astype(BF16)
    qf = _dot(cqn, wq_ref[...])
    for h in range(N_HEADS):
        base = 3 * LANES * h
        lat = qf[:, base:base + LANES]
        rot = rope(qf[:, base + LANES:base + 2 * LANES], qf[:, base + 2 * LANES:base + 3 * LANES])
        q_ref[0, h] = jnp.concatenate([lat, rot], axis=1).astype(BF16)


def _pre_call(x2d, bsz, seq, w):
    tokens = x2d.shape[0]
    tm = min(PRE_TILE, seq)
    seq_tiles = seq // tm
    n_tiles = tokens // tm
    row = lambda i: (i, 0)
    const = lambda i: (0, 0)
    full = lambda a: pl.BlockSpec(a.shape, const)
    weights = [w['ln_in_g'], w['ln_in_b'], w['w_u'], w['w_zs'], w['w_za'], w['w_gs'], w['w_ga'],
               w['w_cq'], w['w_ckv'], w['w_kr'], w['q_norm_g'], w['w_q'], w['kv_norm_g'],
               w['rope_inv'], w['rope_sgn']]
    out_shape = (
        jax.ShapeDtypeStruct((SSM_G // 2, tokens // SSM_CHUNK, SSM_PAIR), BF16),
        jax.ShapeDtypeStruct((tokens, SSM_WIDTH), BF16),
        jax.ShapeDtypeStruct((tokens, ATTN_WIDTH), BF16),
        jax.ShapeDtypeStruct((tokens, D_MODEL), BF16),
        jax.ShapeDtypeStruct((tokens, D_MODEL), BF16),
        jax.ShapeDtypeStruct((bsz, N_HEADS, seq, QK_WIDTH), BF16),
        jax.ShapeDtypeStruct((bsz, seq, QK_WIDTH), BF16),
    )
    out_specs = (
        pl.BlockSpec((SSM_G // 2, tm // SSM_CHUNK, SSM_PAIR), lambda i: (0, i, 0)),
        pl.BlockSpec((tm, SSM_WIDTH), row),
        pl.BlockSpec((tm, ATTN_WIDTH), row),
        pl.BlockSpec((tm, D_MODEL), row),
        pl.BlockSpec((tm, D_MODEL), row),
        pl.BlockSpec((1, N_HEADS, tm, QK_WIDTH), lambda i: (i // seq_tiles, 0, i % seq_tiles, 0)),
        pl.BlockSpec((1, tm, QK_WIDTH), lambda i: (i // seq_tiles, i % seq_tiles, 0)),
    )
    return pl.pallas_call(
        functools.partial(_pre_kernel, seq_tiles=seq_tiles),
        grid=(n_tiles,),
        in_specs=[pl.BlockSpec((tm, D_MODEL), row)] + [full(a) for a in weights],
        out_specs=out_specs,
        out_shape=out_shape,
        scratch_shapes=[pltpu.VMEM((SSM_WIDTH // LANES, tm, LANES), F32)],
        compiler_params=pltpu.CompilerParams(
            dimension_semantics=("arbitrary",), vmem_limit_bytes=VMEM_LIMIT_BYTES),
        name="pre",
    )(x2d, *weights)


def _cmul_add(ar, ai, xr, xi, yr, yi):
    return ar * xr - ai * xi + yr, ar * xi + ai * xr + yi


def _tile_scan(sr, si, cr, ci, pw, reverse):
    row = lax.broadcasted_iota(jnp.int32, (SUBLANES, LANES), 0)
    xr, xi = sr, si
    for k, s in enumerate((1, 2, 4)):
        shift = SUBLANES - s if reverse else s
        keep = (row < SUBLANES - s) if reverse else (row >= s)
        shr = jnp.where(keep, pltpu.roll(xr, shift, 0), 0.0)
        shi = jnp.where(keep, pltpu.roll(xi, shift, 0), 0.0)
        ar, ai = pw(SUBLANES + k)
        xr, xi = _cmul_add(ar, ai, shr, shi, xr, xi)
    pr, pi = pw(slice(0, SUBLANES))
    hr, hi = _cmul_add(pr, pi, cr, ci, xr, xi)
    edge = 0 if reverse else SUBLANES - 1
    out_r = jnp.broadcast_to(hr[edge:edge + 1], (SUBLANES, LANES))
    out_i = jnp.broadcast_to(hi[edge:edge + 1], (SUBLANES, LANES))
    first = (row == SUBLANES - 1) if reverse else (row == 0)
    one = SUBLANES - 1 if reverse else 1
    in_r = jnp.where(first, cr, pltpu.roll(hr, one, 0))
    in_i = jnp.where(first, ci, pltpu.roll(hi, one, 0))
    return in_r, in_i, out_r, out_i


def _s5_kernel(u_ref, t_ref, win_ref, wout_ref, pw_ref, y_ref, s_sc, h_sc, *, nb, nchunk):
    u = u_ref[0]
    s_sc[...] = _dot(u, win_ref[0])
    ntile = nchunk // SUBLANES

    def table(direction):
        base = direction * 2 * S5_PW_ROWS
        def pw(k):
            k = slice(k, k + 1) if isinstance(k, int) else k
            re = pw_ref[0, base + k.start:base + k.stop, :]
            im = pw_ref[0, base + S5_PW_ROWS + k.start:base + S5_PW_ROWS + k.stop, :]
            return re, im
        return pw

    pw_f, pw_b = table(0), table(1)

    def step(t, carry):
        new = []
        for b in range(nb):
            cfr, cfi, cbr, cbi = carry[b]
            rf = pl.multiple_of(b * nchunk + t * SUBLANES, SUBLANES)
            rb = pl.multiple_of(b * nchunk + (ntile - 1 - t) * SUBLANES, SUBLANES)
            rows_f, rows_b = pl.ds(rf, SUBLANES), pl.ds(rb, SUBLANES)
            in_r, in_i, cfr, cfi = _tile_scan(s_sc[rows_f, 0:LANES], s_sc[rows_f, LANES:2 * LANES],
                                              cfr, cfi, pw_f, False)
            h_sc[rows_f, 0:LANES] = in_r
            h_sc[rows_f, LANES:2 * LANES] = in_i
            in_r, in_i, cbr, cbi = _tile_scan(s_sc[rows_b, 2 * LANES:3 * LANES], s_sc[rows_b, 3 * LANES:4 * LANES],
                                              cbr, cbi, pw_b, True)
            h_sc[rows_b, 2 * LANES:3 * LANES] = in_r
            h_sc[rows_b, 3 * LANES:4 * LANES] = in_i
            new.append((cfr, cfi, cbr, cbi))
        return tuple(new)

    zero = jnp.zeros((SUBLANES, LANES), F32)
    lax.fori_loop(0, ntile, step, tuple((zero, zero, zero, zero) for _ in range(nb)))
    y_ref[0] = _dot(u, t_ref[0]) + _dot(h_sc[...].astype(BF16), wout_ref[0])


def _s5_call(u2, nb, nchunk, w):
    npair, rows, width = u2.shape
    blk = lambda i: (i, 0, 0)
    return pl.pallas_call(
        functools.partial(_s5_kernel, nb=nb, nchunk=nchunk),
        grid=(npair,),
        in_specs=[pl.BlockSpec((1, rows, width), blk),
                  pl.BlockSpec((1, width, width), blk),
                  pl.BlockSpec((1, width, width), blk),
                  pl.BlockSpec((1, width, width), blk),
                  pl.BlockSpec((1, 4 * S5_PW_ROWS, LANES), blk)],
        out_specs=pl.BlockSpec((1, rows, width), blk),
        out_shape=jax.ShapeDtypeStruct((npair, rows, width), F32),
        scratch_shapes=[pltpu.VMEM((rows, width), F32), pltpu.VMEM((rows, width), F32)],
        compiler_params=pltpu.CompilerParams(
            dimension_semantics=("arbitrary",), vmem_limit_bytes=VMEM_LIMIT_BYTES),
        name="s5",
    )(u2, w['s5_t'], w['s5_win'], w['s5_wout'], w['s5_pw'])


def _lane_blocks(x, op):
    out = x[:, 0:LANES]
    for c in range(1, x.shape[1] // LANES):
        out = op(out, x[:, c * LANES:(c + 1) * LANES])
    return out


def _attn_kernel(q_ref, k_ref, wuv_ref, o_ref, m_sc, c_sc, l_sc, acc_sc, bad_sc, s_sc, *, tk):
    nh, tq, width = q_ref.shape[1:]
    rows = nh * tq
    nk = k_ref.shape[1] // tk

    def key_rows(kt):
        return pl.ds(pl.multiple_of(kt * tk, tk), tk)

    def scores(kt):
        q = q_ref[0].reshape(rows, width)
        return lax.dot_general(q, k_ref[0, key_rows(kt), :], (((1,), (1,)), ((), ())),
                               preferred_element_type=F32)

    def values(p, kt):
        return _dot(p.astype(BF16), k_ref[0, key_rows(kt), 0:LANES])

    def reset():
        m_sc[...] = jnp.full(m_sc.shape, -1e30, F32)
        l_sc[...] = jnp.zeros(l_sc.shape, F32)
        acc_sc[...] = jnp.zeros(acc_sc.shape, F32)

    reset()
    c_sc[...] = jnp.zeros(c_sc.shape, F32)
    bad_sc[...] = jnp.full(bad_sc.shape, -1.0, F32)

    group = min(ATTN_TILES_PER_TRIP, nk)

    def fast_group(i, carry):
        c = c_sc[...]
        c_wide = jnp.tile(c, (1, tk // LANES))
        l_new, acc_new, m_grp = l_sc[...], acc_sc[...], None
        for j in range(group):
            kt = group * i + j
            s = scores(kt)
            p = jnp.exp2(s - c_wide)
            blk_max = _lane_blocks(s, jnp.maximum)
            m_grp = blk_max if m_grp is None else jnp.maximum(m_grp, blk_max)
            l_new = l_new + _lane_blocks(p, jnp.add)
            acc_new = acc_new + values(p, kt)
        m_next = jnp.maximum(m_sc[...], jnp.max(m_grp, axis=1, keepdims=True))
        shift = m_next - c
        bad_sc[...] = jnp.maximum(bad_sc[...], jnp.abs(shift) - ATTN_OFFSET_SLACK)
        alpha = jnp.exp2(-shift)
        l_sc[...] = alpha * l_new
        acc_sc[...] = alpha * acc_new
        m_sc[...] = m_next
        c_sc[...] = m_next
        return carry

    lax.fori_loop(0, nk // group, fast_group, 0)

    def absorb(kt, carry):
        s_sc[...] = scores(kt)
        s = s_sc[...]
        m_prev = m_sc[...]
        m_next = jnp.maximum(m_prev, jnp.max(s, axis=1, keepdims=True))
        alpha = jnp.exp2(m_prev - m_next)
        p = jnp.exp2(s - jnp.tile(m_next, (1, tk // LANES)))
        l_sc[...] = alpha * l_sc[...] + _lane_blocks(p, jnp.add)
        acc_sc[...] = alpha * acc_sc[...] + values(p, kt)
        m_sc[...] = m_next
        return carry

    @pl.when(jnp.max(bad_sc[...]) > 0.0)
    def _():
        reset()
        lax.fori_loop(0, nk, absorb, 0)

    l_row = jnp.sum(l_sc[...], axis=1, keepdims=True)
    lat = (acc_sc[...] / l_row).astype(BF16)
    outs = [_dot(lat[h * tq:(h + 1) * tq], wuv_ref[h]) for h in range(nh)]
    o_ref[0] = jnp.concatenate(outs, axis=1)


def _attn_call(q, k, wuv):
    bsz, nh, seq, width = q.shape
    tq = min(ATTN_TQ, seq)
    tk = min(ATTN_TK, seq // 2)
    assert seq % tk == 0 and (seq // tk) % min(ATTN_TILES_PER_TRIP, seq // tk) == 0 and seq % tq == 0
    rows = nh * tq
    return pl.pallas_call(
        functools.partial(_attn_kernel, tk=tk),
        grid=(bsz, seq // tq),
        in_specs=[pl.BlockSpec((1, nh, tq, width), lambda b, i: (b, 0, i, 0)),
                  pl.BlockSpec((1, seq, width), lambda b, i: (b, 0, 0)),
                  pl.BlockSpec(wuv.shape, lambda b, i: (0, 0, 0))],
        out_specs=pl.BlockSpec((1, tq, ATTN_WIDTH), lambda b, i: (b, i, 0)),
        out_shape=jax.ShapeDtypeStruct((bsz, seq, ATTN_WIDTH), F32),
        scratch_shapes=[pltpu.VMEM((rows, LANES), F32) for _ in range(5)] + [pltpu.VMEM((rows, tk), F32)],
        compiler_params=pltpu.CompilerParams(
            dimension_semantics=("arbitrary", "arbitrary"), vmem_limit_bytes=VMEM_LIMIT_BYTES),
        name="attn",
    )(q, k, wuv)


def _post_kernel(x_ref, y2_ref, zs_ref, at_ref, za_ref, gs_ref, ga_ref, lig_ref, lib_ref,
                 wglu_ref, bglu_ref, wbs_ref, wba_ref, wo_ref, lng_ref, lnb_ref, o_ref, y_sc):
    xln = _layer_norm(x_ref[...], lig_ref[...], lib_ref[...])
    ys = jax.nn.gelu(_to_token_major(y2_ref, y_sc))
    ys = ys * _sigmoid(_dot(ys.astype(BF16), wglu_ref[...]) + bglu_ref[...])
    zs = zs_ref[...].astype(F32)
    ys = ys * (zs * _sigmoid(zs))
    za = za_ref[...].astype(F32)
    ya = at_ref[...] * (za * _sigmoid(za))
    merged = (_sigmoid(gs_ref[...].astype(F32)) * _dot(ys.astype(BF16), wbs_ref[...])
              + _sigmoid(ga_ref[...].astype(F32)) * _dot(ya.astype(BF16), wba_ref[...]))
    r = ALPHA * xln + _dot(merged.astype(BF16), wo_ref[...])
    o_ref[...] = _layer_norm(r, lng_ref[...], lnb_ref[...])


def _post_call(x2d, y2, zs, at, za, gs, ga, w):
    tokens = x2d.shape[0]
    tm = min(PRE_TILE, tokens)
    row = lambda i: (i, 0)
    const = lambda i: (0, 0)
    full = lambda a: pl.BlockSpec(a.shape, const)
    acts = [x2d, zs, at, za, gs, ga]
    weights = [w['ln_in_g'], w['ln_in_b'], w['w_glu'], w['b_glu'], w['w_bs'], w['w_ba'], w['w_o'],
               w['ln_g'], w['ln_b']]
    return pl.pallas_call(
        _post_kernel,
        grid=(tokens // tm,),
        in_specs=([pl.BlockSpec((tm, D_MODEL), row),
                   pl.BlockSpec((SSM_G // 2, tm // SSM_CHUNK, SSM_PAIR), lambda i: (0, i, 0))]
                  + [pl.BlockSpec((tm, a.shape[1]), row) for a in acts[1:]]
                  + [full(a) for a in weights]),
        out_specs=pl.BlockSpec((tm, D_MODEL), row),
        out_shape=jax.ShapeDtypeStruct((tokens, D_MODEL), F32),
        scratch_shapes=[pltpu.VMEM((SSM_WIDTH // LANES, tm, LANES), F32)],
        compiler_params=pltpu.CompilerParams(
            dimension_semantics=("arbitrary",), vmem_limit_bytes=VMEM_LIMIT_BYTES),
        name="post",
    )(x2d, y2, *acts[1:], *weights)


def _s5_operators(p):
    q = SSM_CHUNK
    gp, n2, c2 = SSM_G // 2, 2 * SSM_N, PAIR_LANES
    w2 = q * c2
    k = jnp.arange(q + 1, dtype=F32)
    same = jnp.eye(2, dtype=F32)

    def pair_states(v):
        return v.reshape(gp, n2)

    def direction(a_re, a_im, log_dt, c_re, c_im):
        a_re, a_im = pair_states(a_re), pair_states(a_im)
        dt = jnp.repeat(jnp.exp(log_dt), SSM_N).reshape(gp, n2)
        er = jnp.exp(a_re * dt)
        lbr, lbi = er * jnp.cos(a_im * dt), er * jnp.sin(a_im * dt)
        den = a_re * a_re + a_im * a_im
        nr, ni = lbr - 1.0, lbi
        cfr = (nr * a_re + ni * a_im) / den
        cfi = (ni * a_re - nr * a_im) / den
        def embed(x):
            x = x.reshape(gp, 2, SSM_P, SSM_N)
            x = x[:, :, :, None, :] * same[None, :, None, :, None]
            return x.reshape(gp, c2, n2)
        b_re = embed(jnp.swapaxes(p['b_re'], 1, 2))
        b_im = embed(jnp.swapaxes(p['b_im'], 1, 2))
        btr = b_re * cfr[:, None, :] - b_im * cfi[:, None, :]
        bti = b_re * cfi[:, None, :] + b_im * cfr[:, None, :]
        cr, ci = embed(c_re), embed(c_im)
        mag = jnp.exp(k[:, None, None] * (a_re * dt)[None])
        ph = k[:, None, None] * (a_im * dt)[None]
        pwr, pwi = mag * jnp.cos(ph), mag * jnp.sin(ph)
        cpr = cr[None] * pwr[:, :, None, :] - ci[None] * pwi[:, :, None, :]
        cpi = cr[None] * pwi[:, :, None, :] + ci[None] * pwr[:, :, None, :]
        pbr = btr[None] * pwr[:, :, None, :] - bti[None] * pwi[:, :, None, :]
        pbi = btr[None] * pwi[:, :, None, :] + bti[None] * pwr[:, :, None, :]
        taps = (jnp.einsum('gcn,tgdn->tgcd', btr, cpr, precision=HIGHEST)
                - jnp.einsum('gcn,tgdn->tgcd', bti, cpi, precision=HIGHEST))
        return taps, (pbr, pbi), (cpr, cpi), (pwr[q], pwi[q])

    tf, pbf, cpf, af = direction(p['a_re_f'], p['a_im_f'], p['log_dt_f'], p['c_re_f'], p['c_im_f'])
    tb, pbb, cpb, ab = direction(p['a_re_b'], p['a_im_b'], p['log_dt_b'], p['c_re_b'], p['c_im_b'])

    skip = jnp.eye(c2, dtype=F32)[None] * p['d_skip'].reshape(gp, 1, c2)
    centre = tf[0] + tb[0] + skip
    lags = jnp.concatenate([tb[q - 1:0:-1], centre[None], tf[1:q], jnp.zeros_like(tf[:1])])
    lags = jnp.transpose(lags, (1, 2, 0, 3)).reshape(gp, c2, 2 * w2).astype(BF16)
    t2 = jnp.stack([lags[:, :, (q - 1 - j) * c2:(q - 1 - j) * c2 + w2] for j in range(q)], axis=1)

    win = jnp.concatenate([pbf[0][q - 1::-1], pbf[1][q - 1::-1], pbb[0][:q], pbb[1][:q]], axis=-1)
    win2 = jnp.transpose(win.astype(BF16), (1, 0, 2, 3))

    wout = jnp.concatenate([cpf[0][1:], -cpf[1][1:], cpb[0][q:0:-1], -cpb[1][q:0:-1]], axis=-1)
    wout2 = jnp.swapaxes(jnp.transpose(wout.astype(BF16), (1, 0, 2, 3)).reshape(gp, w2, 4 * n2), 1, 2)

    def powers(a, reverse):
        ar, ai = a
        pows = [(ar, ai)]
        for _ in range(SUBLANES - 1):
            pr, pi = pows[-1]
            pows.append((pr * ar - pi * ai, pr * ai + pi * ar))
        per_row = pows[::-1] if reverse else pows
        rows = per_row + [pows[0], pows[1], pows[3]]
        rows = rows + [(jnp.zeros_like(ar), jnp.zeros_like(ar))] * (S5_PW_ROWS - len(rows))
        return jnp.concatenate([jnp.stack([r[0] for r in rows], axis=1),
                                jnp.stack([r[1] for r in rows], axis=1)], axis=1)

    pw2 = jnp.concatenate([powers(af, False), powers(ab, True)], axis=1)
    return t2.reshape(gp, w2, w2), win2.reshape(gp, w2, 4 * n2), wout2, pw2


def _prep_weights(ln_in_g, ln_in_b, w_in, ssm_b_re, ssm_b_im, ssm_a_re_fwd, ssm_a_im_fwd,
                  ssm_log_dt_fwd, ssm_c_re_fwd, ssm_c_im_fwd, ssm_a_re_bwd, ssm_a_im_bwd,
                  ssm_log_dt_bwd, ssm_c_re_bwd, ssm_c_im_bwd, ssm_d, w_glu, b_glu, q_norm_g, w_uq,
                  kv_norm_g, w_ukv, w_branch_ssm, w_branch_attn, w_o, ln_g, ln_b):
    l = 0
    w = {}
    row = lambda v: v.reshape(1, -1).astype(F32)
    w['ln_in_g'], w['ln_in_b'] = row(ln_in_g), row(ln_in_b)
    splits = [0]
    for s in IN_SIZES:
        splits.append(splits[-1] + s)
    seg = lambda i: w_in[l][:, splits[i]:splits[i + 1]]
    w['w_u'], w['w_zs'] = seg(0).astype(BF16), seg(1).astype(BF16)
    w['w_cq'], w['w_ckv'] = seg(2).astype(BF16), seg(3).astype(BF16)
    kr = seg(4)
    pad = jnp.zeros((D_MODEL, LANES - QK_ROPE), F32)
    k1, k2 = kr[:, :HALF_ROPE], kr[:, HALF_ROPE:]
    w['w_kr'] = jnp.concatenate([k1, k2, pad, k2, k1, pad], axis=1).astype(BF16)
    w['w_za'], w['w_gs'], w['w_ga'] = seg(5).astype(BF16), seg(6).astype(BF16), seg(7).astype(BF16)

    wuq = w_uq[l].reshape(Q_LORA, N_HEADS, QK_NOPE + QK_ROPE)
    wukv = w_ukv[l].reshape(KV_LORA, N_HEADS, QK_NOPE + V_HEAD)
    w_lat = jnp.einsum('chd,khd->chk', wuq[:, :, :QK_NOPE], wukv[:, :, :QK_NOPE], precision=HIGHEST)
    r1 = wuq[:, :, QK_NOPE:QK_NOPE + HALF_ROPE]
    r2 = wuq[:, :, QK_NOPE + HALF_ROPE:]
    zpad = jnp.zeros((Q_LORA, N_HEADS, LANES - QK_ROPE), F32)
    qscale = (QK_NOPE + QK_ROPE) ** -0.5 * math.log2(math.e)
    wq = jnp.concatenate([w_lat, r1, r2, zpad, r2, r1, zpad], axis=2) * qscale
    w['w_q'] = wq.reshape(Q_LORA, N_HEADS * 3 * LANES).astype(BF16)
    w['w_uv'] = jnp.transpose(wukv[:, :, QK_NOPE:], (1, 0, 2)).astype(BF16)
    w['q_norm_g'], w['kv_norm_g'] = row(q_norm_g[l]), row(kv_norm_g[l])

    inv = ROPE_THETA ** (-jnp.arange(HALF_ROPE, dtype=F32) * 2.0 / QK_ROPE)
    zl = jnp.zeros((LANES - QK_ROPE,), F32)
    w['rope_inv'] = jnp.concatenate([inv, inv, zl]).reshape(1, LANES)
    one = jnp.ones((HALF_ROPE,), F32)
    w['rope_sgn'] = jnp.concatenate([-one, one, zl]).reshape(1, LANES)

    p = dict(b_re=ssm_b_re[l], b_im=ssm_b_im[l], d_skip=ssm_d[l],
             a_re_f=ssm_a_re_fwd[l], a_im_f=ssm_a_im_fwd[l], log_dt_f=ssm_log_dt_fwd[l],
             c_re_f=ssm_c_re_fwd[l], c_im_f=ssm_c_im_fwd[l],
             a_re_b=ssm_a_re_bwd[l], a_im_b=ssm_a_im_bwd[l], log_dt_b=ssm_log_dt_bwd[l],
             c_re_b=ssm_c_re_bwd[l], c_im_b=ssm_c_im_bwd[l])
    w['s5_t'], w['s5_win'], w['s5_wout'], w['s5_pw'] = _s5_operators(p)

    w['w_glu'], w['b_glu'] = w_glu[l].astype(BF16), row(b_glu[l])
    w['w_bs'], w['w_ba'] = w_branch_ssm[l].astype(BF16), w_branch_attn[l].astype(BF16)
    w['w_o'] = w_o[l].astype(BF16)
    w['ln_g'], w['ln_b'] = row(ln_g[l]), row(ln_b[l])
    return w


def _trunk(x, w):
    bsz, seq, _ = x.shape
    tokens = bsz * seq
    x2d = x.reshape(tokens, D_MODEL)
    u2, zs, za, gs, ga, q, k = _pre_call(x2d, bsz, seq, w)
    y2 = _s5_call(u2, bsz, seq // SSM_CHUNK, w)
    at = _attn_call(q, k, w['w_uv']).reshape(tokens, ATTN_WIDTH)
    out = _post_call(x2d, y2, zs, at, za, gs, ga, w)
    return out.reshape(bsz, seq, D_MODEL)


def kernel(x_prompt, x_sample, ln_in_g, ln_in_b, w_in, ssm_b_re, ssm_b_im, ssm_a_re_fwd, ssm_a_im_fwd, ssm_log_dt_fwd, ssm_c_re_fwd, ssm_c_im_fwd, ssm_a_re_bwd, ssm_a_im_bwd, ssm_log_dt_bwd, ssm_c_re_bwd, ssm_c_im_bwd, ssm_d, w_glu, b_glu, q_norm_g, w_uq, kv_norm_g, w_ukv, w_branch_ssm, w_branch_attn, w_o, ln_g, ln_b):
    w = _prep_weights(ln_in_g, ln_in_b, w_in, ssm_b_re, ssm_b_im, ssm_a_re_fwd, ssm_a_im_fwd,
                      ssm_log_dt_fwd, ssm_c_re_fwd, ssm_c_im_fwd, ssm_a_re_bwd, ssm_a_im_bwd,
                      ssm_log_dt_bwd, ssm_c_re_bwd, ssm_c_im_bwd, ssm_d, w_glu, b_glu, q_norm_g,
                      w_uq, kv_norm_g, w_ukv, w_branch_ssm, w_branch_attn, w_o, ln_g, ln_b)
    return (_trunk(x_prompt, w), _trunk(x_sample, w))
```

```python
import functools
import math

import jax
import jax.numpy as jnp
from jax import lax
from jax.experimental import pallas as pl
from jax.experimental.pallas import tpu as pltpu

F32 = jnp.float32
BF16 = jnp.bfloat16
HIGHEST = lax.Precision.HIGHEST

D_MODEL = 1024
DEPTH = 1
SSM_WIDTH = 512
SSM_P = 16
SSM_G = 32
SSM_N = 64
N_HEADS = 8
QK_NOPE = 64
QK_ROPE = 32
V_HEAD = 64
Q_LORA = 256
KV_LORA = 128
ATTN_WIDTH = N_HEADS * V_HEAD
ROPE_THETA = 10000.0
LN_EPS = 1e-5
RMS_EPS = 1e-6
ALPHA = (2.0 * DEPTH) ** 0.25
IN_SIZES = (SSM_WIDTH, SSM_WIDTH, Q_LORA, KV_LORA, QK_ROPE, ATTN_WIDTH, D_MODEL, D_MODEL)

LANES = 128
SUBLANES = 8
S5_PW_ROWS = 16
VMEM_LIMIT_BYTES = 48 * 1024 * 1024

SSM_CHUNK = 16
SSM_PAIR = 2 * SSM_CHUNK * SSM_P
PAIR_LANES = 2 * SSM_P
PAIRS_PER_VREG = LANES // PAIR_LANES
PRE_TILE = 512
ATTN_TQ = 256
ATTN_TK = 1024
ATTN_TILES_PER_TRIP = 4
ATTN_OFFSET_SLACK = 64.0
QK_WIDTH = 2 * LANES
HALF_ROPE = QK_ROPE // 2


def _dot(a, b):
    return jnp.dot(a, b, preferred_element_type=F32)


def _layer_norm(x, g, b):
    mu = jnp.mean(x, axis=-1, keepdims=True)
    xc = x - mu
    var = jnp.mean(xc * xc, axis=-1, keepdims=True)
    return xc * lax.rsqrt(var + LN_EPS) * g + b


def _rms_norm(x, g):
    return x * lax.rsqrt(jnp.mean(x * x, axis=-1, keepdims=True) + RMS_EPS) * g


def _sigmoid(x):
    return 0.5 * jnp.tanh(0.5 * x) + 0.5


def _pair_slot(rows):
    return lax.broadcasted_iota(jnp.int32, (rows, LANES), 1) // PAIR_LANES


def _merge_slots(pieces, shifts, slot):
    out = None
    for k, (x, sh) in enumerate(zip(pieces, shifts)):
        x = pltpu.roll(x, sh % LANES, 1) if sh % LANES else x
        out = x if out is None else jnp.where(slot == k, x, out)
    return out


def _to_chunk_major(u, u_sc, u2_ref):
    rows = u.shape[0] // SSM_CHUNK
    slot = _pair_slot(rows)
    n = PAIRS_PER_VREG
    for b in range(SSM_WIDTH // LANES):
        u_sc[b] = u[:, b * LANES:(b + 1) * LANES]
        for v in range(SSM_PAIR // LANES):
            xs = [u_sc[b, pl.ds(n * v + s, rows, stride=SSM_CHUNK), :] for s in range(n)]
            for q in range(n):
                blk = _merge_slots(xs, [(s - q) * PAIR_LANES for s in range(n)], slot)
                u2_ref[n * b + q, :, v * LANES:(v + 1) * LANES] = blk.astype(u2_ref.dtype)


def _to_token_major(y2_ref, y_sc):
    rows = y2_ref.shape[1]
    slot = _pair_slot(rows)
    n = PAIRS_PER_VREG
    for b in range(SSM_WIDTH // LANES):
        for v in range(SSM_PAIR // LANES):
            ys = [y2_ref[n * b + q, :, v * LANES:(v + 1) * LANES] for q in range(n)]
            for s in range(n):
                blk = _merge_slots(ys, [(q - s) * PAIR_LANES for q in range(n)], slot)
                y_sc[b, pl.ds(n * v + s, rows, stride=SSM_CHUNK), :] = blk
    return jnp.concatenate([y_sc[b] for b in range(SSM_WIDTH // LANES)], axis=1)


def _pre_kernel(x_ref, lng_ref, lnb_ref, wu_ref, wzs_ref, wza_ref, wgs_ref, wga_ref,
                wcq_ref, wckv_ref, wkr_ref, qng_ref, wq_ref, kvng_ref, inv_ref, sgn_ref,
                u2_ref, zs_ref, za_ref, gs_ref, ga_ref, qt_ref, k_ref, vt_ref, u_sc, *, seq_tiles):
    tm = x_ref.shape[0]
    xb = _layer_norm(x_ref[...], lng_ref[...], lnb_ref[...]).astype(BF16)

    cq = _dot(xb, wcq_ref[...])
    ckv = _dot(xb, wckv_ref[...])
    kr = _dot(xb, wkr_ref[...])

    _to_chunk_major(_dot(xb, wu_ref[...]), u_sc, u2_ref)
    zs_ref[...] = _dot(xb, wzs_ref[...]).astype(BF16)
    za_ref[...] = _dot(xb, wza_ref[...]).astype(BF16)
    gs_ref[...] = _dot(xb, wgs_ref[...]).astype(BF16)
    ga_ref[...] = _dot(xb, wga_ref[...]).astype(BF16)

    pos0 = (pl.program_id(0) % seq_tiles) * tm
    pos = (pos0 + lax.broadcasted_iota(jnp.int32, (tm, LANES), 0)).astype(F32)
    ang = pos * inv_ref[...]
    cs = jnp.cos(ang)
    sn = jnp.sin(ang) * sgn_ref[...]

    def rope(a, b):
        return a * cs + b * sn

    kvn = _rms_norm(ckv, kvng_ref[...])
    krot = rope(kr[:, :LANES], kr[:, LANES:])
    k_ref[0] = jnp.concatenate([kvn, krot], axis=1).astype(BF16)
    vt_ref[0] = jnp.transpose(kvn).astype(BF16)

    cqn = _rms_norm(cq, qng_ref[...]).astype(BF16)
    qf = _dot(cqn, wq_ref[...])
    for h in range(N_HEADS):
        base = 3 * LANES * h
        lat = qf[:, base:base + LANES]
        rot = rope(qf[:, base + LANES:base + 2 * LANES], qf[:, base + 2 * LANES:base + 3 * LANES])
        qt_ref[0, h] = jnp.transpose(jnp.concatenate([lat, rot], axis=1)).astype(BF16)


def _pre_call(x2d, bsz, seq, w):
    tokens = x2d.shape[0]
    tm = min(PRE_TILE, seq)
    seq_tiles = seq // tm
    n_tiles = tokens // tm
    row = lambda i: (i, 0)
    const = lambda i: (0, 0)
    full = lambda a: pl.BlockSpec(a.shape, const)
    weights = [w['ln_in_g'], w['ln_in_b'], w['w_u'], w['w_zs'], w['w_za'], w['w_gs'], w['w_ga'],
               w['w_cq'], w['w_ckv'], w['w_kr'], w['q_norm_g'], w['w_q'], w['kv_norm_g'],
               w['rope_inv'], w['rope_sgn']]
    out_shape = (
        jax.ShapeDtypeStruct((SSM_G // 2, tokens // SSM_CHUNK, SSM_PAIR), BF16),
        jax.ShapeDtypeStruct((tokens, SSM_WIDTH), BF16),
        jax.ShapeDtypeStruct((tokens, ATTN_WIDTH), BF16),
        jax.ShapeDtypeStruct((tokens, D_MODEL), BF16),
        jax.ShapeDtypeStruct((tokens, D_MODEL), BF16),
        jax.ShapeDtypeStruct((bsz, N_HEADS, QK_WIDTH, seq), BF16),
        jax.ShapeDtypeStruct((bsz, seq, QK_WIDTH), BF16),
        jax.ShapeDtypeStruct((bsz, KV_LORA, seq), BF16),
    )
    out_specs = (
        pl.BlockSpec((SSM_G // 2, tm // SSM_CHUNK, SSM_PAIR), lambda i: (0, i, 0)),
        pl.BlockSpec((tm, SSM_WIDTH), row),
        pl.BlockSpec((tm, ATTN_WIDTH), row),
        pl.BlockSpec((tm, D_MODEL), row),
        pl.BlockSpec((tm, D_MODEL), row),
        pl.BlockSpec((1, N_HEADS, QK_WIDTH, tm), lambda i: (i // seq_tiles, 0, 0, i % seq_tiles)),
        pl.BlockSpec((1, tm, QK_WIDTH), lambda i: (i // seq_tiles, i % seq_tiles, 0)),
        pl.BlockSpec((1, KV_LORA, tm), lambda i: (i // seq_tiles, 0, i % seq_tiles)),
    )
    return pl.pallas_call(
        functools.partial(_pre_kernel, seq_tiles=seq_tiles),
        grid=(n_tiles,),
        in_specs=[pl.BlockSpec((tm, D_MODEL), row)] + [full(a) for a in weights],
        out_specs=out_specs,
        out_shape=out_shape,
        scratch_shapes=[pltpu.VMEM((SSM_WIDTH // LANES, tm, LANES), F32)],
        compiler_params=pltpu.CompilerParams(
            dimension_semantics=("arbitrary",), vmem_limit_bytes=VMEM_LIMIT_BYTES),
        name="pre",
    )(x2d, *weights)


def _cmul_add(ar, ai, xr, xi, yr, yi):
    return ar * xr - ai * xi + yr, ar * xi + ai * xr + yi


def _tile_scan(sr, si, cr, ci, pw, reverse):
    row = lax.broadcasted_iota(jnp.int32, (SUBLANES, LANES), 0)
    xr, xi = sr, si
    for k, s in enumerate((1, 2, 4)):
        shift = SUBLANES - s if reverse else s
        keep = (row < SUBLANES - s) if reverse else (row >= s)
        shr = jnp.where(keep, pltpu.roll(xr, shift, 0), 0.0)
        shi = jnp.where(keep, pltpu.roll(xi, shift, 0), 0.0)
        ar, ai = pw(SUBLANES + k)
        xr, xi = _cmul_add(ar, ai, shr, shi, xr, xi)
    pr, pi = pw(slice(0, SUBLANES))
    hr, hi = _cmul_add(pr, pi, cr, ci, xr, xi)
    edge = 0 if reverse else SUBLANES - 1
    out_r = jnp.broadcast_to(hr[edge:edge + 1], (SUBLANES, LANES))
    out_i = jnp.broadcast_to(hi[edge:edge + 1], (SUBLANES, LANES))
    first = (row == SUBLANES - 1) if reverse else (row == 0)
    one = SUBLANES - 1 if reverse else 1
    in_r = jnp.where(first, cr, pltpu.roll(hr, one, 0))
    in_i = jnp.where(first, ci, pltpu.roll(hi, one, 0))
    return in_r, in_i, out_r, out_i


def _s5_kernel(u_ref, t_ref, win_ref, wout_ref, pw_ref, y_ref, s_sc, h_sc, *, nb, nchunk):
    u = u_ref[0]
    s_sc[...] = _dot(u, win_ref[0])
    ntile = nchunk // SUBLANES

    def table(direction):
        base = direction * 2 * S5_PW_ROWS
        def pw(k):
            k = slice(k, k + 1) if isinstance(k, int) else k
            re = pw_ref[0, base + k.start:base + k.stop, :]
            im = pw_ref[0, base + S5_PW_ROWS + k.start:base + S5_PW_ROWS + k.stop, :]
            return re, im
        return pw

    pw_f, pw_b = table(0), table(1)

    def step(t, carry):
        new = []
        for b in range(nb):
            cfr, cfi, cbr, cbi = carry[b]
            rf = pl.multiple_of(b * nchunk + t * SUBLANES, SUBLANES)
            rb = pl.multiple_of(b * nchunk + (ntile - 1 - t) * SUBLANES, SUBLANES)
            rows_f, rows_b = pl.ds(rf, SUBLANES), pl.ds(rb, SUBLANES)
            in_r, in_i, cfr, cfi = _tile_scan(s_sc[rows_f, 0:LANES], s_sc[rows_f, LANES:2 * LANES],
                                              cfr, cfi, pw_f, False)
            h_sc[rows_f, 0:LANES] = in_r
            h_sc[rows_f, LANES:2 * LANES] = in_i
            in_r, in_i, cbr, cbi = _tile_scan(s_sc[rows_b, 2 * LANES:3 * LANES], s_sc[rows_b, 3 * LANES:4 * LANES],
                                              cbr, cbi, pw_b, True)
            h_sc[rows_b, 2 * LANES:3 * LANES] = in_r
            h_sc[rows_b, 3 * LANES:4 * LANES] = in_i
            new.append((cfr, cfi, cbr, cbi))
        return tuple(new)

    zero = jnp.zeros((SUBLANES, LANES), F32)
    lax.fori_loop(0, ntile, step, tuple((zero, zero, zero, zero) for _ in range(nb)))
    y_ref[0] = _dot(u, t_ref[0]) + _dot(h_sc[...].astype(BF16), wout_ref[0])


def _s5_call(u2, nb, nchunk, w):
    npair, rows, width = u2.shape
    blk = lambda i: (i, 0, 0)
    return pl.pallas_call(
        functools.partial(_s5_kernel, nb=nb, nchunk=nchunk),
        grid=(npair,),
        in_specs=[pl.BlockSpec((1, rows, width), blk),
                  pl.BlockSpec((1, width, width), blk),
                  pl.BlockSpec((1, width, width), blk),
                  pl.BlockSpec((1, width, width), blk),
                  pl.BlockSpec((1, 4 * S5_PW_ROWS, LANES), blk)],
        out_specs=pl.BlockSpec((1, rows, width), blk),
        out_shape=jax.ShapeDtypeStruct((npair, rows, width), F32),
        scratch_shapes=[pltpu.VMEM((rows, width), F32), pltpu.VMEM((rows, width), F32)],
        compiler_params=pltpu.CompilerParams(
            dimension_semantics=("arbitrary",), vmem_limit_bytes=VMEM_LIMIT_BYTES),
        name="s5",
    )(u2, w['s5_t'], w['s5_win'], w['s5_wout'], w['s5_pw'])


def _row_blocks(x, op):
    n = x.shape[0] // SUBLANES
    accs = [x[r * SUBLANES:(r + 1) * SUBLANES] for r in range(min(4, n))]
    for r in range(len(accs), n):
        accs[r % 4] = op(accs[r % 4], x[r * SUBLANES:(r + 1) * SUBLANES])
    out = accs[0]
    for a in accs[1:]:
        out = op(out, a)
    return out


def _spread(v, n):
    return jnp.concatenate([v] * (n // SUBLANES), axis=0)


def _attn_kernel(qt_ref, k_ref, vt_ref, wuvt_ref, o_ref, m_sc, c_sc, l_sc, bad_sc, acc_sc, s_sc, *, tk):
    nh, width, tq = qt_ref.shape[1:]
    nk = k_ref.shape[1] // tk
    latent = vt_ref.shape[1]

    def key_rows(kt):
        return pl.ds(pl.multiple_of(kt * tk, tk), tk)

    def scores(kt):
        qt = jnp.concatenate([qt_ref[0, h] for h in range(nh)], axis=1)
        return _dot(k_ref[0, key_rows(kt), :], qt)

    def values(p, kt):
        return _dot(vt_ref[0, :, key_rows(kt)], p.astype(BF16))

    def reset():
        m_sc[...] = jnp.full(m_sc.shape, -1e30, F32)
        l_sc[...] = jnp.zeros(l_sc.shape, F32)
        acc_sc[...] = jnp.zeros(acc_sc.shape, F32)

    reset()
    c_sc[...] = jnp.zeros(c_sc.shape, F32)
    bad_sc[...] = jnp.full(bad_sc.shape, -1.0, F32)

    group = min(ATTN_TILES_PER_TRIP, nk)

    def fast_group(i, carry):
        c = c_sc[...]
        c_wide = _spread(c, tk)
        l_new, acc_new, m_grp = l_sc[...], acc_sc[...], None
        for j in range(group):
            kt = group * i + j
            s = scores(kt)
            p = jnp.exp2(s - c_wide)
            blk_max = _row_blocks(s, jnp.maximum)
            m_grp = blk_max if m_grp is None else jnp.maximum(m_grp, blk_max)
            l_new = l_new + _row_blocks(p, jnp.add)
            acc_new = acc_new + values(p, kt)
        m_next = jnp.maximum(m_sc[...], jnp.max(m_grp, axis=0, keepdims=True))
        shift = m_next - c
        bad_sc[...] = jnp.maximum(bad_sc[...], jnp.abs(shift) - ATTN_OFFSET_SLACK)
        alpha = jnp.exp2(-shift)
        l_sc[...] = alpha * l_new
        acc_sc[...] = _spread(alpha, latent) * acc_new
        m_sc[...] = m_next
        c_sc[...] = m_next
        return carry

    lax.fori_loop(0, nk // group, fast_group, 0)

    def absorb(kt, carry):
        s_sc[...] = scores(kt)
        s = s_sc[...]
        m_prev = m_sc[...]
        m_next = jnp.maximum(m_prev, jnp.max(_row_blocks(s, jnp.maximum), axis=0, keepdims=True))
        alpha = jnp.exp2(m_prev - m_next)
        p = jnp.exp2(s - _spread(m_next, tk))
        l_sc[...] = alpha * l_sc[...] + _row_blocks(p, jnp.add)
        acc_sc[...] = _spread(alpha, latent) * acc_sc[...] + values(p, kt)
        m_sc[...] = m_next
        return carry

    @pl.when(jnp.max(bad_sc[...]) > 0.0)
    def _():
        reset()
        lax.fori_loop(0, nk, absorb, 0)

    l_col = jnp.sum(l_sc[...], axis=0, keepdims=True)
    latt = (acc_sc[...] / l_col).astype(BF16)
    outs = [_dot(wuvt_ref[h], latt[:, h * tq:(h + 1) * tq]) for h in range(nh)]
    o_ref[0] = jnp.transpose(jnp.concatenate(outs, axis=0))


def _attn_call(qt, k, vt, wuvt):
    bsz, nh, width, seq = qt.shape
    tq = min(ATTN_TQ, seq)
    tk = min(ATTN_TK, seq // 2)
    assert seq % tk == 0 and (seq // tk) % min(ATTN_TILES_PER_TRIP, seq // tk) == 0 and seq % tq == 0
    cols = nh * tq
    stat = pltpu.VMEM((SUBLANES, cols), F32)
    return pl.pallas_call(
        functools.partial(_attn_kernel, tk=tk),
        grid=(bsz, seq // tq),
        in_specs=[pl.BlockSpec((1, nh, width, tq), lambda b, i: (b, 0, 0, i)),
                  pl.BlockSpec((1, seq, width), lambda b, i: (b, 0, 0)),
                  pl.BlockSpec((1, vt.shape[1], seq), lambda b, i: (b, 0, 0)),
                  pl.BlockSpec(wuvt.shape, lambda b, i: (0, 0, 0))],
        out_specs=pl.BlockSpec((1, tq, ATTN_WIDTH), lambda b, i: (b, i, 0)),
        out_shape=jax.ShapeDtypeStruct((bsz, seq, ATTN_WIDTH), F32),
        scratch_shapes=[stat, stat, stat, stat, pltpu.VMEM((vt.shape[1], cols), F32),
                        pltpu.VMEM((tk, cols), F32)],
        compiler_params=pltpu.CompilerParams(
            dimension_semantics=("arbitrary", "arbitrary"), vmem_limit_bytes=VMEM_LIMIT_BYTES),
        name="attn",
    )(qt, k, vt, wuvt)


def _post_kernel(x_ref, y2_ref, zs_ref, at_ref, za_ref, gs_ref, ga_ref, lig_ref, lib_ref,
                 wglu_ref, bglu_ref, wbs_ref, wba_ref, wo_ref, lng_ref, lnb_ref, o_ref, y_sc):
    xln = _layer_norm(x_ref[...], lig_ref[...], lib_ref[...])
    ys = jax.nn.gelu(_to_token_major(y2_ref, y_sc))
    ys = ys * _sigmoid(_dot(ys.astype(BF16), wglu_ref[...]) + bglu_ref[...])
    zs = zs_ref[...].astype(F32)
    ys = ys * (zs * _sigmoid(zs))
    za = za_ref[...].astype(F32)
    ya = at_ref[...] * (za * _sigmoid(za))
    merged = (_sigmoid(gs_ref[...].astype(F32)) * _dot(ys.astype(BF16), wbs_ref[...])
              + _sigmoid(ga_ref[...].astype(F32)) * _dot(ya.astype(BF16), wba_ref[...]))
    r = ALPHA * xln + _dot(merged.astype(BF16), wo_ref[...])
    o_ref[...] = _layer_norm(r, lng_ref[...], lnb_ref[...])


def _post_call(x2d, y2, zs, at, za, gs, ga, w):
    tokens = x2d.shape[0]
    tm = min(PRE_TILE, tokens)
    row = lambda i: (i, 0)
    const = lambda i: (0, 0)
    full = lambda a: pl.BlockSpec(a.shape, const)
    acts = [x2d, zs, at, za, gs, ga]
    weights = [w['ln_in_g'], w['ln_in_b'], w['w_glu'], w['b_glu'], w['w_bs'], w['w_ba'], w['w_o'],
               w['ln_g'], w['ln_b']]
    return pl.pallas_call(
        _post_kernel,
        grid=(tokens // tm,),
        in_specs=([pl.BlockSpec((tm, D_MODEL), row),
                   pl.BlockSpec((SSM_G // 2, tm // SSM_CHUNK, SSM_PAIR), lambda i: (0, i, 0))]
                  + [pl.BlockSpec((tm, a.shape[1]), row) for a in acts[1:]]
                  + [full(a) for a in weights]),
        out_specs=pl.BlockSpec((tm, D_MODEL), row),
        out_shape=jax.ShapeDtypeStruct((tokens, D_MODEL), F32),
        scratch_shapes=[pltpu.VMEM((SSM_WIDTH // LANES, tm, LANES), F32)],
        compiler_params=pltpu.CompilerParams(
            dimension_semantics=("arbitrary",), vmem_limit_bytes=VMEM_LIMIT_BYTES),
        name="post",
    )(x2d, y2, *acts[1:], *weights)


def _s5_operators(p):
    q = SSM_CHUNK
    gp, n2, c2 = SSM_G // 2, 2 * SSM_N, PAIR_LANES
    w2 = q * c2
    k = jnp.arange(q + 1, dtype=F32)
    same = jnp.eye(2, dtype=F32)

    def pair_states(v):
        return v.reshape(gp, n2)

    def direction(a_re, a_im, log_dt, c_re, c_im):
        a_re, a_im = pair_states(a_re), pair_states(a_im)
        dt = jnp.repeat(jnp.exp(log_dt), SSM_N).reshape(gp, n2)
        er = jnp.exp(a_re * dt)
        lbr, lbi = er * jnp.cos(a_im * dt), er * jnp.sin(a_im * dt)
        den = a_re * a_re + a_im * a_im
        nr, ni = lbr - 1.0, lbi
        cfr = (nr * a_re + ni * a_im) / den
        cfi = (ni * a_re - nr * a_im) / den
        def embed(x):
            x = x.reshape(gp, 2, SSM_P, SSM_N)
            x = x[:, :, :, None, :] * same[None, :, None, :, None]
            return x.reshape(gp, c2, n2)
        b_re = embed(jnp.swapaxes(p['b_re'], 1, 2))
        b_im = embed(jnp.swapaxes(p['b_im'], 1, 2))
        btr = b_re * cfr[:, None, :] - b_im * cfi[:, None, :]
        bti = b_re * cfi[:, None, :] + b_im * cfr[:, None, :]
        cr, ci = embed(c_re), embed(c_im)
        mag = jnp.exp(k[:, None, None] * (a_re * dt)[None])
        ph = k[:, None, None] * (a_im * dt)[None]
        pwr, pwi = mag * jnp.cos(ph), mag * jnp.sin(ph)
        cpr = cr[None] * pwr[:, :, None, :] - ci[None] * pwi[:, :, None, :]
        cpi = cr[None] * pwi[:, :, None, :] + ci[None] * pwr[:, :, None, :]
        pbr = btr[None] * pwr[:, :, None, :] - bti[None] * pwi[:, :, None, :]
        pbi = btr[None] * pwi[:, :, None, :] + bti[None] * pwr[:, :, None, :]
        taps = (jnp.einsum('gcn,tgdn->tgcd', btr, cpr, precision=HIGHEST)
                - jnp.einsum('gcn,tgdn->tgcd', bti, cpi, precision=HIGHEST))
        return taps, (pbr, pbi), (cpr, cpi), (pwr[q], pwi[q])

    tf, pbf, cpf, af = direction(p['a_re_f'], p['a_im_f'], p['log_dt_f'], p['c_re_f'], p['c_im_f'])
    tb, pbb, cpb, ab = direction(p['a_re_b'], p['a_im_b'], p['log_dt_b'], p['c_re_b'], p['c_im_b'])

    skip = jnp.eye(c2, dtype=F32)[None] * p['d_skip'].reshape(gp, 1, c2)
    centre = tf[0] + tb[0] + skip
    lags = jnp.concatenate([tb[q - 1:0:-1], centre[None], tf[1:q], jnp.zeros_like(tf[:1])])
    lags = jnp.transpose(lags, (1, 2, 0, 3)).reshape(gp, c2, 2 * w2).astype(BF16)
    t2 = jnp.stack([lags[:, :, (q - 1 - j) * c2:(q - 1 - j) * c2 + w2] for j in range(q)], axis=1)

    win = jnp.concatenate([pbf[0][q - 1::-1], pbf[1][q - 1::-1], pbb[0][:q], pbb[1][:q]], axis=-1)
    win2 = jnp.transpose(win.astype(BF16), (1, 0, 2, 3))

    wout = jnp.concatenate([cpf[0][1:], -cpf[1][1:], cpb[0][q:0:-1], -cpb[1][q:0:-1]], axis=-1)
    wout2 = jnp.swapaxes(jnp.transpose(wout.astype(BF16), (1, 0, 2, 3)).reshape(gp, w2, 4 * n2), 1, 2)

    def powers(a, reverse):
        ar, ai = a
        pows = [(ar, ai)]
        for _ in range(SUBLANES - 1):
            pr, pi = pows[-1]
            pows.append((pr * ar - pi * ai, pr * ai + pi * ar))
        per_row = pows[::-1] if reverse else pows
        rows = per_row + [pows[0], pows[1], pows[3]]
        rows = rows + [(jnp.zeros_like(ar), jnp.zeros_like(ar))] * (S5_PW_ROWS - len(rows))
        return jnp.concatenate([jnp.stack([r[0] for r in rows], axis=1),
                                jnp.stack([r[1] for r in rows], axis=1)], axis=1)

    pw2 = jnp.concatenate([powers(af, False), powers(ab, True)], axis=1)
    return t2.reshape(gp, w2, w2), win2.reshape(gp, w2, 4 * n2), wout2, pw2


def _prep_weights(ln_in_g, ln_in_b, w_in, ssm_b_re, ssm_b_im, ssm_a_re_fwd, ssm_a_im_fwd,
                  ssm_log_dt_fwd, ssm_c_re_fwd, ssm_c_im_fwd, ssm_a_re_bwd, ssm_a_im_bwd,
                  ssm_log_dt_bwd, ssm_c_re_bwd, ssm_c_im_bwd, ssm_d, w_glu, b_glu, q_norm_g, w_uq,
                  kv_norm_g, w_ukv, w_branch_ssm, w_branch_attn, w_o, ln_g, ln_b):
    l = 0
    w = {}
    row = lambda v: v.reshape(1, -1).astype(F32)
    w['ln_in_g'], w['ln_in_b'] = row(ln_in_g), row(ln_in_b)
    splits = [0]
    for s in IN_SIZES:
        splits.append(splits[-1] + s)
    seg = lambda i: w_in[l][:, splits[i]:splits[i + 1]]
    w['w_u'], w['w_zs'] = seg(0).astype(BF16), seg(1).astype(BF16)
    w['w_cq'], w['w_ckv'] = seg(2).astype(BF16), seg(3).astype(BF16)
    kr = seg(4)
    pad = jnp.zeros((D_MODEL, LANES - QK_ROPE), F32)
    k1, k2 = kr[:, :HALF_ROPE], kr[:, HALF_ROPE:]
    w['w_kr'] = jnp.concatenate([k1, k2, pad, k2, k1, pad], axis=1).astype(BF16)
    w['w_za'], w['w_gs'], w['w_ga'] = seg(5).astype(BF16), seg(6).astype(BF16), seg(7).astype(BF16)

    wuq = w_uq[l].reshape(Q_LORA, N_HEADS, QK_NOPE + QK_ROPE)
    wukv = w_ukv[l].reshape(KV_LORA, N_HEADS, QK_NOPE + V_HEAD)
    w_lat = jnp.einsum('chd,khd->chk', wuq[:, :, :QK_NOPE], wukv[:, :, :QK_NOPE], precision=HIGHEST)
    r1 = wuq[:, :, QK_NOPE:QK_NOPE + HALF_ROPE]
    r2 = wuq[:, :, QK_NOPE + HALF_ROPE:]
    zpad = jnp.zeros((Q_LORA, N_HEADS, LANES - QK_ROPE), F32)
    qscale = (QK_NOPE + QK_ROPE) ** -0.5 * math.log2(math.e)
    wq = jnp.concatenate([w_lat, r1, r2, zpad, r2, r1, zpad], axis=2) * qscale
    w['w_q'] = wq.reshape(Q_LORA, N_HEADS * 3 * LANES).astype(BF16)
    w['w_uvt'] = jnp.transpose(wukv[:, :, QK_NOPE:], (1, 2, 0)).astype(BF16)
    w['q_norm_g'], w['kv_norm_g'] = row(q_norm_g[l]), row(kv_norm_g[l])

    inv = ROPE_THETA ** (-jnp.arange(HALF_ROPE, dtype=F32) * 2.0 / QK_ROPE)
    zl = jnp.zeros((LANES - QK_ROPE,), F32)
    w['rope_inv'] = jnp.concatenate([inv, inv, zl]).reshape(1, LANES)
    one = jnp.ones((HALF_ROPE,), F32)
    w['rope_sgn'] = jnp.concatenate([-one, one, zl]).reshape(1, LANES)

    p = dict(b_re=ssm_b_re[l], b_im=ssm_b_im[l], d_skip=ssm_d[l],
             a_re_f=ssm_a_re_fwd[l], a_im_f=ssm_a_im_fwd[l], log_dt_f=ssm_log_dt_fwd[l],
             c_re_f=ssm_c_re_fwd[l], c_im_f=ssm_c_im_fwd[l],
             a_re_b=ssm_a_re_bwd[l], a_im_b=ssm_a_im_bwd[l], log_dt_b=ssm_log_dt_bwd[l],
             c_re_b=ssm_c_re_bwd[l], c_im_b=ssm_c_im_bwd[l])
    w['s5_t'], w['s5_win'], w['s5_wout'], w['s5_pw'] = _s5_operators(p)

    w['w_glu'], w['b_glu'] = w_glu[l].astype(BF16), row(b_glu[l])
    w['w_bs'], w['w_ba'] = w_branch_ssm[l].astype(BF16), w_branch_attn[l].astype(BF16)
    w['w_o'] = w_o[l].astype(BF16)
    w['ln_g'], w['ln_b'] = row(ln_g[l]), row(ln_b[l])
    return w


def _trunk(x, w):
    bsz, seq, _ = x.shape
    tokens = bsz * seq
    x2d = x.reshape(tokens, D_MODEL)
    u2, zs, za, gs, ga, qt, k, vt = _pre_call(x2d, bsz, seq, w)
    y2 = _s5_call(u2, bsz, seq // SSM_CHUNK, w)
    at = _attn_call(qt, k, vt, w['w_uvt']).reshape(tokens, ATTN_WIDTH)
    out = _post_call(x2d, y2, zs, at, za, gs, ga, w)
    return out.reshape(bsz, seq, D_MODEL)


def kernel(x_prompt, x_sample, ln_in_g, ln_in_b, w_in, ssm_b_re, ssm_b_im, ssm_a_re_fwd, ssm_a_im_fwd, ssm_log_dt_fwd, ssm_c_re_fwd, ssm_c_im_fwd, ssm_a_re_bwd, ssm_a_im_bwd, ssm_log_dt_bwd, ssm_c_re_bwd, ssm_c_im_bwd, ssm_d, w_glu, b_glu, q_norm_g, w_uq, kv_norm_g, w_ukv, w_branch_ssm, w_branch_attn, w_o, ln_g, ln_b):
    w = _prep_weights(ln_in_g, ln_in_b, w_in, ssm_b_re, ssm_b_im, ssm_a_re_fwd, ssm_a_im_fwd,
                      ssm_log_dt_fwd, ssm_c_re_fwd, ssm_c_im_fwd, ssm_a_re_bwd, ssm_a_im_bwd,
                      ssm_log_dt_bwd, ssm_c_re_bwd, ssm_c_im_bwd, ssm_d, w_glu, b_glu, q_norm_g,
                      w_uq, kv_norm_g, w_ukv, w_branch_ssm, w_branch_attn, w_o, ln_g, ln_b)
    return (_trunk(x_prompt, w), _trunk(x_sample, w))
```

```python
import functools
import math

import jax
import jax.numpy as jnp
from jax import lax
from jax.experimental import pallas as pl
from jax.experimental.pallas import tpu as pltpu

F32 = jnp.float32
BF16 = jnp.bfloat16
HIGHEST = lax.Precision.HIGHEST

D_MODEL = 1024
DEPTH = 1
SSM_WIDTH = 512
SSM_P = 16
SSM_G = 32
SSM_N = 64
N_HEADS = 8
QK_NOPE = 64
QK_ROPE = 32
V_HEAD = 64
Q_LORA = 256
KV_LORA = 128
ATTN_WIDTH = N_HEADS * V_HEAD
ROPE_THETA = 10000.0
LN_EPS = 1e-5
RMS_EPS = 1e-6
ALPHA = (2.0 * DEPTH) ** 0.25
IN_SIZES = (SSM_WIDTH, SSM_WIDTH, Q_LORA, KV_LORA, QK_ROPE, ATTN_WIDTH, D_MODEL, D_MODEL)

LANES = 128
SUBLANES = 8
S5_PW_ROWS = 16
VMEM_LIMIT_BYTES = 48 * 1024 * 1024

SSM_CHUNK = 16
SSM_PAIR = 2 * SSM_CHUNK * SSM_P
PAIR_LANES = 2 * SSM_P
PAIRS_PER_VREG = LANES // PAIR_LANES
PRE_TILE = 512
ATTN_TQ = 256
ATTN_TK = 1024
ATTN_TILES_PER_TRIP = 4
ATTN_OFFSET_SLACK = 64.0
QK_WIDTH = 2 * LANES
HALF_ROPE = QK_ROPE // 2


def _dot(a, b):
    return jnp.dot(a, b, preferred_element_type=F32)


def _layer_norm(x, g, b):
    mu = jnp.mean(x, axis=-1, keepdims=True)
    xc = x - mu
    var = jnp.mean(xc * xc, axis=-1, keepdims=True)
    return xc * lax.rsqrt(var + LN_EPS) * g + b


def _rms_norm(x, g):
    return x * lax.rsqrt(jnp.mean(x * x, axis=-1, keepdims=True) + RMS_EPS) * g


def _sigmoid(x):
    return 0.5 * jnp.tanh(0.5 * x) + 0.5


def _pair_slot(rows):
    return lax.broadcasted_iota(jnp.int32, (rows, LANES), 1) // PAIR_LANES


def _merge_slots(pieces, shifts, slot):
    out = None
    for k, (x, sh) in enumerate(zip(pieces, shifts)):
        x = pltpu.roll(x, sh % LANES, 1) if sh % LANES else x
        out = x if out is None else jnp.where(slot == k, x, out)
    return out


def _to_chunk_major(u, u_sc, u2_ref):
    rows = u.shape[0] // SSM_CHUNK
    slot = _pair_slot(rows)
    n = PAIRS_PER_VREG
    for b in range(SSM_WIDTH // LANES):
        u_sc[b] = u[:, b * LANES:(b + 1) * LANES]
        for v in range(SSM_PAIR // LANES):
            xs = [u_sc[b, pl.ds(n * v + s, rows, stride=SSM_CHUNK), :] for s in range(n)]
            for q in range(n):
                blk = _merge_slots(xs, [(s - q) * PAIR_LANES for s in range(n)], slot)
                u2_ref[n * b + q, :, v * LANES:(v + 1) * LANES] = blk.astype(u2_ref.dtype)


def _to_token_major(y2_ref, y_sc):
    rows = y2_ref.shape[1]
    slot = _pair_slot(rows)
    n = PAIRS_PER_VREG
    for b in range(SSM_WIDTH // LANES):
        for v in range(SSM_PAIR // LANES):
            ys = [y2_ref[n * b + q, :, v * LANES:(v + 1) * LANES] for q in range(n)]
            for s in range(n):
                blk = _merge_slots(ys, [(q - s) * PAIR_LANES for q in range(n)], slot)
                y_sc[b, pl.ds(n * v + s, rows, stride=SSM_CHUNK), :] = blk
    return jnp.concatenate([y_sc[b] for b in range(SSM_WIDTH // LANES)], axis=1)


def _pre_kernel(x_ref, lng_ref, lnb_ref, wu_ref, wzs_ref, wza_ref, wgs_ref, wga_ref,
                wcq_ref, wckv_ref, wkr_ref, qng_ref, wq_ref, kvng_ref, inv_ref, sgn_ref,
                u2_ref, zs_ref, za_ref, gs_ref, ga_ref, qt_ref, k_ref, vt_ref, xa_ref, u_sc, *, seq_tiles):
    tm = x_ref.shape[0]
    xln = _layer_norm(x_ref[...], lng_ref[...], lnb_ref[...])
    xa_ref[...] = ALPHA * xln
    xb = xln.astype(BF16)

    cq = _dot(xb, wcq_ref[...])
    ckv = _dot(xb, wckv_ref[...])
    kr = _dot(xb, wkr_ref[...])

    _to_chunk_major(_dot(xb, wu_ref[...]), u_sc, u2_ref)
    zs = _dot(xb, wzs_ref[...])
    zs_ref[...] = (zs * _sigmoid(zs)).astype(BF16)
    za = _dot(xb, wza_ref[...])
    za_ref[...] = (za * _sigmoid(za)).astype(BF16)
    gs_ref[...] = _sigmoid(_dot(xb, wgs_ref[...])).astype(BF16)
    ga_ref[...] = _sigmoid(_dot(xb, wga_ref[...])).astype(BF16)

    pos0 = (pl.program_id(0) % seq_tiles) * tm
    pos = (pos0 + lax.broadcasted_iota(jnp.int32, (tm, LANES), 0)).astype(F32)
    ang = pos * inv_ref[...]
    cs = jnp.cos(ang)
    sn = jnp.sin(ang) * sgn_ref[...]

    def rope(a, b):
        return a * cs + b * sn

    kvn = _rms_norm(ckv, kvng_ref[...])
    krot = rope(kr[:, :LANES], kr[:, LANES:])
    k_ref[0] = jnp.concatenate([kvn, krot], axis=1).astype(BF16)
    vt_ref[0] = jnp.transpose(kvn).astype(BF16)

    cqn = _rms_norm(cq, qng_ref[...]).astype(BF16)
    qf = _dot(cqn, wq_ref[...])
    for h in range(N_HEADS):
        base = 3 * LANES * h
        lat = qf[:, base:base + LANES]
        rot = rope(qf[:, base + LANES:base + 2 * LANES], qf[:, base + 2 * LANES:base + 3 * LANES])
        qt_ref[0, h] = jnp.transpose(jnp.concatenate([lat, rot], axis=1)).astype(BF16)


def _pre_call(x2d, bsz, seq, w):
    tokens = x2d.shape[0]
    tm = min(PRE_TILE, seq)
    seq_tiles = seq // tm
    n_tiles = tokens // tm
    row = lambda i: (i, 0)
    const = lambda i: (0, 0)
    full = lambda a: pl.BlockSpec(a.shape, const)
    weights = [w['ln_in_g'], w['ln_in_b'], w['w_u'], w['w_zs'], w['w_za'], w['w_gs'], w['w_ga'],
               w['w_cq'], w['w_ckv'], w['w_kr'], w['q_norm_g'], w['w_q'], w['kv_norm_g'],
               w['rope_inv'], w['rope_sgn']]
    out_shape = (
        jax.ShapeDtypeStruct((SSM_G // 2, tokens // SSM_CHUNK, SSM_PAIR), BF16),
        jax.ShapeDtypeStruct((tokens, SSM_WIDTH), BF16),
        jax.ShapeDtypeStruct((tokens, ATTN_WIDTH), BF16),
        jax.ShapeDtypeStruct((tokens, D_MODEL), BF16),
        jax.ShapeDtypeStruct((tokens, D_MODEL), BF16),
        jax.ShapeDtypeStruct((bsz, N_HEADS, QK_WIDTH, seq), BF16),
        jax.ShapeDtypeStruct((bsz, seq, QK_WIDTH), BF16),
        jax.ShapeDtypeStruct((bsz, KV_LORA, seq), BF16),
        jax.ShapeDtypeStruct((tokens, D_MODEL), F32),
    )
    out_specs = (
        pl.BlockSpec((SSM_G // 2, tm // SSM_CHUNK, SSM_PAIR), lambda i: (0, i, 0)),
        pl.BlockSpec((tm, SSM_WIDTH), row),
        pl.BlockSpec((tm, ATTN_WIDTH), row),
        pl.BlockSpec((tm, D_MODEL), row),
        pl.BlockSpec((tm, D_MODEL), row),
        pl.BlockSpec((1, N_HEADS, QK_WIDTH, tm), lambda i: (i // seq_tiles, 0, 0, i % seq_tiles)),
        pl.BlockSpec((1, tm, QK_WIDTH), lambda i: (i // seq_tiles, i % seq_tiles, 0)),
        pl.BlockSpec((1, KV_LORA, tm), lambda i: (i // seq_tiles, 0, i % seq_tiles)),
        pl.BlockSpec((tm, D_MODEL), row),
    )
    return pl.pallas_call(
        functools.partial(_pre_kernel, seq_tiles=seq_tiles),
        grid=(n_tiles,),
        in_specs=[pl.BlockSpec((tm, D_MODEL), row)] + [full(a) for a in weights],
        out_specs=out_specs,
        out_shape=out_shape,
        scratch_shapes=[pltpu.VMEM((SSM_WIDTH // LANES, tm, LANES), F32)],
        compiler_params=pltpu.CompilerParams(
            dimension_semantics=("arbitrary",), vmem_limit_bytes=VMEM_LIMIT_BYTES),
        name="pre",
    )(x2d, *weights)


def _cmul_add(ar, ai, xr, xi, yr, yi):
    return ar * xr - ai * xi + yr, ar * xi + ai * xr + yi


def _tile_scan(sr, si, cr, ci, pw, reverse):
    row = lax.broadcasted_iota(jnp.int32, (SUBLANES, LANES), 0)
    xr, xi = sr, si
    for k, s in enumerate((1, 2, 4)):
        shift = SUBLANES - s if reverse else s
        keep = (row < SUBLANES - s) if reverse else (row >= s)
        shr = jnp.where(keep, pltpu.roll(xr, shift, 0), 0.0)
        shi = jnp.where(keep, pltpu.roll(xi, shift, 0), 0.0)
        ar, ai = pw(SUBLANES + k)
        xr, xi = _cmul_add(ar, ai, shr, shi, xr, xi)
    pr, pi = pw(slice(0, SUBLANES))
    hr, hi = _cmul_add(pr, pi, cr, ci, xr, xi)
    edge = 0 if reverse else SUBLANES - 1
    out_r = jnp.broadcast_to(hr[edge:edge + 1], (SUBLANES, LANES))
    out_i = jnp.broadcast_to(hi[edge:edge + 1], (SUBLANES, LANES))
    first = (row == SUBLANES - 1) if reverse else (row == 0)
    one = SUBLANES - 1 if reverse else 1
    in_r = jnp.where(first, cr, pltpu.roll(hr, one, 0))
    in_i = jnp.where(first, ci, pltpu.roll(hi, one, 0))
    return in_r, in_i, out_r, out_i


def _s5_kernel(u_ref, t_ref, win_ref, wout_ref, pw_ref, y_ref, s_sc, h_sc, *, nb, nchunk):
    u = u_ref[0]
    s_sc[...] = _dot(u, win_ref[0])
    ntile = nchunk // SUBLANES

    def table(direction):
        base = direction * 2 * S5_PW_ROWS
        def pw(k):
            k = slice(k, k + 1) if isinstance(k, int) else k
            re = pw_ref[0, base + k.start:base + k.stop, :]
            im = pw_ref[0, base + S5_PW_ROWS + k.start:base + S5_PW_ROWS + k.stop, :]
            return re, im
        return pw

    pw_f, pw_b = table(0), table(1)

    def step(t, carry):
        new = []
        for b in range(nb):
            cfr, cfi, cbr, cbi = carry[b]
            rf = pl.multiple_of(b * nchunk + t * SUBLANES, SUBLANES)
            rb = pl.multiple_of(b * nchunk + (ntile - 1 - t) * SUBLANES, SUBLANES)
            rows_f, rows_b = pl.ds(rf, SUBLANES), pl.ds(rb, SUBLANES)
            in_r, in_i, cfr, cfi = _tile_scan(s_sc[rows_f, 0:LANES], s_sc[rows_f, LANES:2 * LANES],
                                              cfr, cfi, pw_f, False)
            h_sc[rows_f, 0:LANES] = in_r
            h_sc[rows_f, LANES:2 * LANES] = in_i
            in_r, in_i, cbr, cbi = _tile_scan(s_sc[rows_b, 2 * LANES:3 * LANES], s_sc[rows_b, 3 * LANES:4 * LANES],
                                              cbr, cbi, pw_b, True)
            h_sc[rows_b, 2 * LANES:3 * LANES] = in_r
            h_sc[rows_b, 3 * LANES:4 * LANES] = in_i
            new.append((cfr, cfi, cbr, cbi))
        return tuple(new)

    zero = jnp.zeros((SUBLANES, LANES), F32)
    lax.fori_loop(0, ntile, step, tuple((zero, zero, zero, zero) for _ in range(nb)))
    y_ref[0] = _dot(u, t_ref[0]) + _dot(h_sc[...].astype(BF16), wout_ref[0])


def _s5_call(u2, nb, nchunk, w):
    npair, rows, width = u2.shape
    blk = lambda i: (i, 0, 0)
    return pl.pallas_call(
        functools.partial(_s5_kernel, nb=nb, nchunk=nchunk),
        grid=(npair,),
        in_specs=[pl.BlockSpec((1, rows, width), blk),
                  pl.BlockSpec((1, width, width), blk),
                  pl.BlockSpec((1, width, width), blk),
                  pl.BlockSpec((1, width, width), blk),
                  pl.BlockSpec((1, 4 * S5_PW_ROWS, LANES), blk)],
        out_specs=pl.BlockSpec((1, rows, width), blk),
        out_shape=jax.ShapeDtypeStruct((npair, rows, width), F32),
        scratch_shapes=[pltpu.VMEM((rows, width), F32), pltpu.VMEM((rows, width), F32)],
        compiler_params=pltpu.CompilerParams(
            dimension_semantics=("arbitrary",), vmem_limit_bytes=VMEM_LIMIT_BYTES),
        name="s5",
    )(u2, w['s5_t'], w['s5_win'], w['s5_wout'], w['s5_pw'])


def _row_blocks(x, op):
    n = x.shape[0] // SUBLANES
    accs = [x[r * SUBLANES:(r + 1) * SUBLANES] for r in range(min(4, n))]
    for r in range(len(accs), n):
        accs[r % 4] = op(accs[r % 4], x[r * SUBLANES:(r + 1) * SUBLANES])
    out = accs[0]
    for a in accs[1:]:
        out = op(out, a)
    return out


def _spread(v, n):
    return jnp.concatenate([v] * (n // SUBLANES), axis=0)


def _attn_kernel(qt_ref, k_ref, vt_ref, wuvt_ref, o_ref, m_sc, c_sc, l_sc, bad_sc, acc_sc, s_sc, *, tk):
    nh, width, tq = qt_ref.shape[1:]
    nk = k_ref.shape[1] // tk
    latent = vt_ref.shape[1]

    def key_rows(kt):
        return pl.ds(pl.multiple_of(kt * tk, tk), tk)

    def scores(kt):
        qt = jnp.concatenate([qt_ref[0, h] for h in range(nh)], axis=1)
        return _dot(k_ref[0, key_rows(kt), :], qt)

    def values(p, kt):
        return _dot(vt_ref[0, :, key_rows(kt)], p.astype(BF16))

    def reset():
        m_sc[...] = jnp.full(m_sc.shape, -1e30, F32)
        l_sc[...] = jnp.zeros(l_sc.shape, F32)
        acc_sc[...] = jnp.zeros(acc_sc.shape, F32)

    reset()
    c_sc[...] = jnp.zeros(c_sc.shape, F32)
    bad_sc[...] = jnp.full(bad_sc.shape, -1.0, F32)

    group = min(ATTN_TILES_PER_TRIP, nk)

    def fast_group(i, carry):
        c = c_sc[...]
        c_wide = _spread(c, tk)
        l_new, acc_new, m_grp = l_sc[...], acc_sc[...], None
        for j in range(group):
            kt = group * i + j
            s = scores(kt)
            p = jnp.exp2(s - c_wide)
            blk_max = _row_blocks(s, jnp.maximum)
            m_grp = blk_max if m_grp is None else jnp.maximum(m_grp, blk_max)
            l_new = l_new + _row_blocks(p, jnp.add)
            acc_new = acc_new + values(p, kt)
        m_next = jnp.maximum(m_sc[...], jnp.max(m_grp, axis=0, keepdims=True))
        shift = m_next - c
        bad_sc[...] = jnp.maximum(bad_sc[...], jnp.abs(shift) - ATTN_OFFSET_SLACK)
        alpha = jnp.exp2(-shift)
        l_sc[...] = alpha * l_new
        acc_sc[...] = _spread(alpha, latent) * acc_new
        m_sc[...] = m_next
        c_sc[...] = m_next
        return carry

    lax.fori_loop(0, nk // group, fast_group, 0)

    def absorb(kt, carry):
        s_sc[...] = scores(kt)
        s = s_sc[...]
        m_prev = m_sc[...]
        m_next = jnp.maximum(m_prev, jnp.max(_row_blocks(s, jnp.maximum), axis=0, keepdims=True))
        alpha = jnp.exp2(m_prev - m_next)
        p = jnp.exp2(s - _spread(m_next, tk))
        l_sc[...] = alpha * l_sc[...] + _row_blocks(p, jnp.add)
        acc_sc[...] = _spread(alpha, latent) * acc_sc[...] + values(p, kt)
        m_sc[...] = m_next
        return carry

    @pl.when(jnp.max(bad_sc[...]) > 0.0)
    def _():
        reset()
        lax.fori_loop(0, nk, absorb, 0)

    l_col = jnp.sum(l_sc[...], axis=0, keepdims=True)
    latt = (acc_sc[...] / l_col).astype(BF16)
    outs = [_dot(wuvt_ref[h], latt[:, h * tq:(h + 1) * tq]) for h in range(nh)]
    o_ref[0] = jnp.transpose(jnp.concatenate(outs, axis=0))


def _attn_call(qt, k, vt, wuvt):
    bsz, nh, width, seq = qt.shape
    tq = min(ATTN_TQ, seq)
    tk = min(ATTN_TK, seq // 2)
    assert seq % tk == 0 and (seq // tk) % min(ATTN_TILES_PER_TRIP, seq // tk) == 0 and seq % tq == 0
    cols = nh * tq
    stat = pltpu.VMEM((SUBLANES, cols), F32)
    return pl.pallas_call(
        functools.partial(_attn_kernel, tk=tk),
        grid=(bsz, seq // tq),
        in_specs=[pl.BlockSpec((1, nh, width, tq), lambda b, i: (b, 0, 0, i)),
                  pl.BlockSpec((1, seq, width), lambda b, i: (b, 0, 0)),
                  pl.BlockSpec((1, vt.shape[1], seq), lambda b, i: (b, 0, 0)),
                  pl.BlockSpec(wuvt.shape, lambda b, i: (0, 0, 0))],
        out_specs=pl.BlockSpec((1, tq, ATTN_WIDTH), lambda b, i: (b, i, 0)),
        out_shape=jax.ShapeDtypeStruct((bsz, seq, ATTN_WIDTH), F32),
        scratch_shapes=[stat, stat, stat, stat, pltpu.VMEM((vt.shape[1], cols), F32),
                        pltpu.VMEM((tk, cols), F32)],
        compiler_params=pltpu.CompilerParams(
            dimension_semantics=("arbitrary", "arbitrary"), vmem_limit_bytes=VMEM_LIMIT_BYTES),
        name="attn",
    )(qt, k, vt, wuvt)


def _post_kernel(xa_ref, y2_ref, zs_ref, at_ref, za_ref, gs_ref, ga_ref,
                 wglu_ref, bglu_ref, wbs_ref, wba_ref, wo_ref, lng_ref, lnb_ref, o_ref, y_sc):
    y = _to_token_major(y2_ref, y_sc)
    half = y.shape[0] // 2
    for r0 in (0, half):
        rows = slice(r0, r0 + half)
        ys = jax.nn.gelu(y[rows])
        ys = ys * _sigmoid(_dot(ys.astype(BF16), wglu_ref[...]) + bglu_ref[...])
        ys = ys * zs_ref[rows, :].astype(F32)
        ya = at_ref[rows, :] * za_ref[rows, :].astype(F32)
        merged = (gs_ref[rows, :].astype(F32) * _dot(ys.astype(BF16), wbs_ref[...])
                  + ga_ref[rows, :].astype(F32) * _dot(ya.astype(BF16), wba_ref[...]))
        r = xa_ref[rows, :] + _dot(merged.astype(BF16), wo_ref[...])
        o_ref[rows, :] = _layer_norm(r, lng_ref[...], lnb_ref[...])


def _post_call(xa, y2, zs, at, za, gs, ga, w):
    tokens = xa.shape[0]
    tm = min(PRE_TILE, tokens)
    row = lambda i: (i, 0)
    const = lambda i: (0, 0)
    full = lambda a: pl.BlockSpec(a.shape, const)
    acts = [xa, zs, at, za, gs, ga]
    weights = [w['w_glu'], w['b_glu'], w['w_bs'], w['w_ba'], w['w_o'], w['ln_g'], w['ln_b']]
    return pl.pallas_call(
        _post_kernel,
        grid=(tokens // tm,),
        in_specs=([pl.BlockSpec((tm, D_MODEL), row),
                   pl.BlockSpec((SSM_G // 2, tm // SSM_CHUNK, SSM_PAIR), lambda i: (0, i, 0))]
                  + [pl.BlockSpec((tm, a.shape[1]), row) for a in acts[1:]]
                  + [full(a) for a in weights]),
        out_specs=pl.BlockSpec((tm, D_MODEL), row),
        out_shape=jax.ShapeDtypeStruct((tokens, D_MODEL), F32),
        scratch_shapes=[pltpu.VMEM((SSM_WIDTH // LANES, tm, LANES), F32)],
        compiler_params=pltpu.CompilerParams(
            dimension_semantics=("arbitrary",), vmem_limit_bytes=VMEM_LIMIT_BYTES),
        name="post",
    )(xa, y2, *acts[1:], *weights)


def _s5_operators(p):
    q = SSM_CHUNK
    gp, n2, c2 = SSM_G // 2, 2 * SSM_N, PAIR_LANES
    w2 = q * c2
    k = jnp.arange(q + 1, dtype=F32)
    same = jnp.eye(2, dtype=F32)

    def pair_states(v):
        return v.reshape(gp, n2)

    def direction(a_re, a_im, log_dt, c_re, c_im):
        a_re, a_im = pair_states(a_re), pair_states(a_im)
        dt = jnp.repeat(jnp.exp(log_dt), SSM_N).reshape(gp, n2)
        er = jnp.exp(a_re * dt)
        lbr, lbi = er * jnp.cos(a_im * dt), er * jnp.sin(a_im * dt)
        den = a_re * a_re + a_im * a_im
        nr, ni = lbr - 1.0, lbi
        cfr = (nr * a_re + ni * a_im) / den
        cfi = (ni * a_re - nr * a_im) / den
        def embed(x):
            x = x.reshape(gp, 2, SSM_P, SSM_N)
            x = x[:, :, :, None, :] * same[None, :, None, :, None]
            return x.reshape(gp, c2, n2)
        b_re = embed(jnp.swapaxes(p['b_re'], 1, 2))
        b_im = embed(jnp.swapaxes(p['b_im'], 1, 2))
        btr = b_re * cfr[:, None, :] - b_im * cfi[:, None, :]
        bti = b_re * cfi[:, None, :] + b_im * cfr[:, None, :]
        cr, ci = embed(c_re), embed(c_im)
        mag = jnp.exp(k[:, None, None] * (a_re * dt)[None])
        ph = k[:, None, None] * (a_im * dt)[None]
        pwr, pwi = mag * jnp.cos(ph), mag * jnp.sin(ph)
        cpr = cr[None] * pwr[:, :, None, :] - ci[None] * pwi[:, :, None, :]
        cpi = cr[None] * pwi[:, :, None, :] + ci[None] * pwr[:, :, None, :]
        pbr = btr[None] * pwr[:, :, None, :] - bti[None] * pwi[:, :, None, :]
        pbi = btr[None] * pwi[:, :, None, :] + bti[None] * pwr[:, :, None, :]
        taps = (jnp.einsum('gcn,tgdn->tgcd', btr, cpr, precision=HIGHEST)
                - jnp.einsum('gcn,tgdn->tgcd', bti, cpi, precision=HIGHEST))
        return taps, (pbr, pbi), (cpr, cpi), (pwr[q], pwi[q])

    tf, pbf, cpf, af = direction(p['a_re_f'], p['a_im_f'], p['log_dt_f'], p['c_re_f'], p['c_im_f'])
    tb, pbb, cpb, ab = direction(p['a_re_b'], p['a_im_b'], p['log_dt_b'], p['c_re_b'], p['c_im_b'])

    skip = jnp.eye(c2, dtype=F32)[None] * p['d_skip'].reshape(gp, 1, c2)
    centre = tf[0] + tb[0] + skip
    lags = jnp.concatenate([tb[q - 1:0:-1], centre[None], tf[1:q], jnp.zeros_like(tf[:1])])
    lags = jnp.transpose(lags, (1, 2, 0, 3)).reshape(gp, c2, 2 * w2).astype(BF16)
    t2 = jnp.stack([lags[:, :, (q - 1 - j) * c2:(q - 1 - j) * c2 + w2] for j in range(q)], axis=1)

    win = jnp.concatenate([pbf[0][q - 1::-1], pbf[1][q - 1::-1], pbb[0][:q], pbb[1][:q]], axis=-1)
    win2 = jnp.transpose(win.astype(BF16), (1, 0, 2, 3))

    wout = jnp.concatenate([cpf[0][1:], -cpf[1][1:], cpb[0][q:0:-1], -cpb[1][q:0:-1]], axis=-1)
    wout2 = jnp.swapaxes(jnp.transpose(wout.astype(BF16), (1, 0, 2, 3)).reshape(gp, w2, 4 * n2), 1, 2)

    def powers(a, reverse):
        ar, ai = a
        pows = [(ar, ai)]
        for _ in range(SUBLANES - 1):
            pr, pi = pows[-1]
            pows.append((pr * ar - pi * ai, pr * ai + pi * ar))
        per_row = pows[::-1] if reverse else pows
        rows = per_row + [pows[0], pows[1], pows[3]]
        rows = rows + [(jnp.zeros_like(ar), jnp.zeros_like(ar))] * (S5_PW_ROWS - len(rows))
        return jnp.concatenate([jnp.stack([r[0] for r in rows], axis=1),
                                jnp.stack([r[1] for r in rows], axis=1)], axis=1)

    pw2 = jnp.concatenate([powers(af, False), powers(ab, True)], axis=1)
    return t2.reshape(gp, w2, w2), win2.reshape(gp, w2, 4 * n2), wout2, pw2


def _prep_weights(ln_in_g, ln_in_b, w_in, ssm_b_re, ssm_b_im, ssm_a_re_fwd, ssm_a_im_fwd,
                  ssm_log_dt_fwd, ssm_c_re_fwd, ssm_c_im_fwd, ssm_a_re_bwd, ssm_a_im_bwd,
                  ssm_log_dt_bwd, ssm_c_re_bwd, ssm_c_im_bwd, ssm_d, w_glu, b_glu, q_norm_g, w_uq,
                  kv_norm_g, w_ukv, w_branch_ssm, w_branch_attn, w_o, ln_g, ln_b):
    l = 0
    w = {}
    row = lambda v: v.reshape(1, -1).astype(F32)
    w['ln_in_g'], w['ln_in_b'] = row(ln_in_g), row(ln_in_b)
    splits = [0]
    for s in IN_SIZES:
        splits.append(splits[-1] + s)
    seg = lambda i: w_in[l][:, splits[i]:splits[i + 1]]
    w['w_u'], w['w_zs'] = seg(0).astype(BF16), seg(1).astype(BF16)
    w['w_cq'], w['w_ckv'] = seg(2).astype(BF16), seg(3).astype(BF16)
    kr = seg(4)
    pad = jnp.zeros((D_MODEL, LANES - QK_ROPE), F32)
    k1, k2 = kr[:, :HALF_ROPE], kr[:, HALF_ROPE:]
    w['w_kr'] = jnp.concatenate([k1, k2, pad, k2, k1, pad], axis=1).astype(BF16)
    w['w_za'], w['w_gs'], w['w_ga'] = seg(5).astype(BF16), seg(6).astype(BF16), seg(7).astype(BF16)

    wuq = w_uq[l].reshape(Q_LORA, N_HEADS, QK_NOPE + QK_ROPE)
    wukv = w_ukv[l].reshape(KV_LORA, N_HEADS, QK_NOPE + V_HEAD)
    w_lat = jnp.einsum('chd,khd->chk', wuq[:, :, :QK_NOPE], wukv[:, :, :QK_NOPE], precision=HIGHEST)
    r1 = wuq[:, :, QK_NOPE:QK_NOPE + HALF_ROPE]
    r2 = wuq[:, :, QK_NOPE + HALF_ROPE:]
    zpad = jnp.zeros((Q_LORA, N_HEADS, LANES - QK_ROPE), F32)
    qscale = (QK_NOPE + QK_ROPE) ** -0.5 * math.log2(math.e)
    wq = jnp.concatenate([w_lat, r1, r2, zpad, r2, r1, zpad], axis=2) * qscale
    w['w_q'] = wq.reshape(Q_LORA, N_HEADS * 3 * LANES).astype(BF16)
    w['w_uvt'] = jnp.transpose(wukv[:, :, QK_NOPE:], (1, 2, 0)).astype(BF16)
    w['q_norm_g'], w['kv_norm_g'] = row(q_norm_g[l]), row(kv_norm_g[l])

    inv = ROPE_THETA ** (-jnp.arange(HALF_ROPE, dtype=F32) * 2.0 / QK_ROPE)
    zl = jnp.zeros((LANES - QK_ROPE,), F32)
    w['rope_inv'] = jnp.concatenate([inv, inv, zl]).reshape(1, LANES)
    one = jnp.ones((HALF_ROPE,), F32)
    w['rope_sgn'] = jnp.concatenate([-one, one, zl]).reshape(1, LANES)

    p = dict(b_re=ssm_b_re[l], b_im=ssm_b_im[l], d_skip=ssm_d[l],
             a_re_f=ssm_a_re_fwd[l], a_im_f=ssm_a_im_fwd[l], log_dt_f=ssm_log_dt_fwd[l],
             c_re_f=ssm_c_re_fwd[l], c_im_f=ssm_c_im_fwd[l],
             a_re_b=ssm_a_re_bwd[l], a_im_b=ssm_a_im_bwd[l], log_dt_b=ssm_log_dt_bwd[l],
             c_re_b=ssm_c_re_bwd[l], c_im_b=ssm_c_im_bwd[l])
    w['s5_t'], w['s5_win'], w['s5_wout'], w['s5_pw'] = _s5_operators(p)

    w['w_glu'], w['b_glu'] = w_glu[l].astype(BF16), row(b_glu[l])
    w['w_bs'], w['w_ba'] = w_branch_ssm[l].astype(BF16), w_branch_attn[l].astype(BF16)
    w['w_o'] = w_o[l].astype(BF16)
    w['ln_g'], w['ln_b'] = row(ln_g[l]), row(ln_b[l])
    return w


def _trunk(x, w):
    bsz, seq, _ = x.shape
    tokens = bsz * seq
    x2d = x.reshape(tokens, D_MODEL)
    u2, zs, za, gs, ga, qt, k, vt, xa = _pre_call(x2d, bsz, seq, w)
    y2 = _s5_call(u2, bsz, seq // SSM_CHUNK, w)
    at = _attn_call(qt, k, vt, w['w_uvt']).reshape(tokens, ATTN_WIDTH)
    out = _post_call(xa, y2, zs, at, za, gs, ga, w)
    return out.reshape(bsz, seq, D_MODEL)


def kernel(x_prompt, x_sample, ln_in_g, ln_in_b, w_in, ssm_b_re, ssm_b_im, ssm_a_re_fwd, ssm_a_im_fwd, ssm_log_dt_fwd, ssm_c_re_fwd, ssm_c_im_fwd, ssm_a_re_bwd, ssm_a_im_bwd, ssm_log_dt_bwd, ssm_c_re_bwd, ssm_c_im_bwd, ssm_d, w_glu, b_glu, q_norm_g, w_uq, kv_norm_g, w_ukv, w_branch_ssm, w_branch_attn, w_o, ln_g, ln_b):
    w = _prep_weights(ln_in_g, ln_in_b, w_in, ssm_b_re, ssm_b_im, ssm_a_re_fwd, ssm_a_im_fwd,
                      ssm_log_dt_fwd, ssm_c_re_fwd, ssm_c_im_fwd, ssm_a_re_bwd, ssm_a_im_bwd,
                      ssm_log_dt_bwd, ssm_c_re_bwd, ssm_c_im_bwd, ssm_d, w_glu, b_glu, q_norm_g,
                      w_uq, kv_norm_g, w_ukv, w_branch_ssm, w_branch_attn, w_o, ln_g, ln_b)
    return (_trunk(x_prompt, w), _trunk(x_sample, w))
```

```python
import functools
import math

import jax
import jax.numpy as jnp
from jax import lax
from jax.experimental import pallas as pl
from jax.experimental.pallas import tpu as pltpu

F32 = jnp.float32
BF16 = jnp.bfloat16
HIGHEST = lax.Precision.HIGHEST

D_MODEL = 1024
DEPTH = 1
SSM_WIDTH = 512
SSM_P = 16
SSM_G = 32
SSM_N = 64
N_HEADS = 8
QK_NOPE = 64
QK_ROPE = 32
V_HEAD = 64
Q_LORA = 256
KV_LORA = 128
ATTN_WIDTH = N_HEADS * V_HEAD
ROPE_THETA = 10000.0
LN_EPS = 1e-5
RMS_EPS = 1e-6
ALPHA = (2.0 * DEPTH) ** 0.25
IN_SIZES = (SSM_WIDTH, SSM_WIDTH, Q_LORA, KV_LORA, QK_ROPE, ATTN_WIDTH, D_MODEL, D_MODEL)

LANES = 128
SUBLANES = 8
S5_PW_ROWS = 16
VMEM_LIMIT_BYTES = 48 * 1024 * 1024

SSM_CHUNK = 16
SSM_PAIR = 2 * SSM_CHUNK * SSM_P
PAIR_LANES = 2 * SSM_P
PAIRS_PER_VREG = LANES // PAIR_LANES
PRE_TILE = 512
ATTN_TQ = 256
ATTN_TK = 1024
ATTN_TILES_PER_TRIP = 4
ATTN_OFFSET_SLACK = 64.0
QK_WIDTH = 2 * LANES
HALF_ROPE = QK_ROPE // 2


def _dot(a, b):
    return jnp.dot(a, b, preferred_element_type=F32)


def _layer_norm(x, g, b):
    mu = jnp.mean(x, axis=-1, keepdims=True)
    xc = x - mu
    var = jnp.mean(xc * xc, axis=-1, keepdims=True)
    return xc * lax.rsqrt(var + LN_EPS) * g + b


def _rms_norm(x, g):
    return x * lax.rsqrt(jnp.mean(x * x, axis=-1, keepdims=True) + RMS_EPS) * g


def _sigmoid(x):
    return 0.5 * jnp.tanh(0.5 * x) + 0.5


def _pair_slot(rows):
    return lax.broadcasted_iota(jnp.int32, (rows, LANES), 1) // PAIR_LANES


def _merge_slots(pieces, shifts, slot):
    out = None
    for k, (x, sh) in enumerate(zip(pieces, shifts)):
        x = pltpu.roll(x, sh % LANES, 1) if sh % LANES else x
        out = x if out is None else jnp.where(slot == k, x, out)
    return out


def _to_chunk_major(u, u_sc, u2_ref):
    rows = u.shape[0] // SSM_CHUNK
    slot = _pair_slot(rows)
    n = PAIRS_PER_VREG
    for b in range(SSM_WIDTH // LANES):
        u_sc[b] = u[:, b * LANES:(b + 1) * LANES]
        for v in range(SSM_PAIR // LANES):
            xs = [u_sc[b, pl.ds(n * v + s, rows, stride=SSM_CHUNK), :] for s in range(n)]
            for q in range(n):
                blk = _merge_slots(xs, [(s - q) * PAIR_LANES for s in range(n)], slot)
                u2_ref[n * b + q, :, v * LANES:(v + 1) * LANES] = blk.astype(u2_ref.dtype)


def _to_token_major(y2_ref, y_sc):
    rows = y2_ref.shape[1]
    slot = _pair_slot(rows)
    n = PAIRS_PER_VREG
    for b in range(SSM_WIDTH // LANES):
        for v in range(SSM_PAIR // LANES):
            ys = [y2_ref[n * b + q, :, v * LANES:(v + 1) * LANES] for q in range(n)]
            for s in range(n):
                blk = _merge_slots(ys, [(q - s) * PAIR_LANES for q in range(n)], slot)
                y_sc[b, pl.ds(n * v + s, rows, stride=SSM_CHUNK), :] = blk
    return jnp.concatenate([y_sc[b] for b in range(SSM_WIDTH // LANES)], axis=1)


def _pre_kernel(x_ref, lng_ref, lnb_ref, wu_ref, wzs_ref, wza_ref, wgs_ref, wga_ref,
                wcq_ref, wckv_ref, wkr_ref, qng_ref, wq_ref, kvng_ref, inv_ref, sgn_ref,
                u2_ref, zs_ref, za_ref, gs_ref, ga_ref, qt_ref, k_ref, vt_ref, xa_ref, u_sc, tab_sc,
                *, seq_tiles):
    tm = x_ref.shape[0]

    @pl.when(pl.program_id(0) == 0)
    def _():
        rel = lax.broadcasted_iota(jnp.int32, (tm, LANES), 0).astype(F32) * inv_ref[...]
        tab_sc[0] = jnp.cos(rel)
        tab_sc[1] = jnp.sin(rel)

    xln = _layer_norm(x_ref[...], lng_ref[...], lnb_ref[...])
    xa_ref[...] = ALPHA * xln
    xb = xln.astype(BF16)

    cq = _dot(xb, wcq_ref[...])
    ckv = _dot(xb, wckv_ref[...])
    kr = _dot(xb, wkr_ref[...])

    _to_chunk_major(_dot(xb, wu_ref[...]), u_sc, u2_ref)
    zs = _dot(xb, wzs_ref[...])
    zs_ref[...] = (zs * _sigmoid(zs)).astype(BF16)
    za = _dot(xb, wza_ref[...])
    za_ref[...] = (za * _sigmoid(za)).astype(BF16)
    gs_ref[...] = _sigmoid(_dot(xb, wgs_ref[...])).astype(BF16)
    ga_ref[...] = _sigmoid(_dot(xb, wga_ref[...])).astype(BF16)

    pos0 = ((pl.program_id(0) % seq_tiles) * tm).astype(F32)
    base = jnp.broadcast_to(pos0 * inv_ref[...], (SUBLANES, LANES))
    cb, sb = jnp.cos(base)[0:1], jnp.sin(base)[0:1]
    cs = cb * tab_sc[0] - sb * tab_sc[1]
    sn = (sb * tab_sc[0] + cb * tab_sc[1]) * sgn_ref[...]

    def rope(a, b):
        return a * cs + b * sn

    kvn = _rms_norm(ckv, kvng_ref[...])
    krot = rope(kr[:, :LANES], kr[:, LANES:])
    k_ref[0] = jnp.concatenate([kvn, krot], axis=1).astype(BF16)
    vt_ref[0] = jnp.transpose(kvn).astype(BF16)

    cqn = _rms_norm(cq, qng_ref[...]).astype(BF16)
    qf = _dot(cqn, wq_ref[...])
    for h in range(N_HEADS):
        base = 3 * LANES * h
        lat = qf[:, base:base + LANES]
        rot = rope(qf[:, base + LANES:base + 2 * LANES], qf[:, base + 2 * LANES:base + 3 * LANES])
        qt_ref[0, h] = jnp.transpose(jnp.concatenate([lat, rot], axis=1)).astype(BF16)


def _pre_call(x2d, bsz, seq, w):
    tokens = x2d.shape[0]
    tm = min(PRE_TILE, seq)
    seq_tiles = seq // tm
    n_tiles = tokens // tm
    row = lambda i: (i, 0)
    const = lambda i: (0, 0)
    full = lambda a: pl.BlockSpec(a.shape, const)
    weights = [w['ln_in_g'], w['ln_in_b'], w['w_u'], w['w_zs'], w['w_za'], w['w_gs'], w['w_ga'],
               w['w_cq'], w['w_ckv'], w['w_kr'], w['q_norm_g'], w['w_q'], w['kv_norm_g'],
               w['rope_inv'], w['rope_sgn']]
    out_shape = (
        jax.ShapeDtypeStruct((SSM_G // 2, tokens // SSM_CHUNK, SSM_PAIR), BF16),
        jax.ShapeDtypeStruct((tokens, SSM_WIDTH), BF16),
        jax.ShapeDtypeStruct((tokens, ATTN_WIDTH), BF16),
        jax.ShapeDtypeStruct((tokens, D_MODEL), BF16),
        jax.ShapeDtypeStruct((tokens, D_MODEL), BF16),
        jax.ShapeDtypeStruct((bsz, N_HEADS, QK_WIDTH, seq), BF16),
        jax.ShapeDtypeStruct((bsz, seq, QK_WIDTH), BF16),
        jax.ShapeDtypeStruct((bsz, KV_LORA, seq), BF16),
        jax.ShapeDtypeStruct((tokens, D_MODEL), F32),
    )
    out_specs = (
        pl.BlockSpec((SSM_G // 2, tm // SSM_CHUNK, SSM_PAIR), lambda i: (0, i, 0)),
        pl.BlockSpec((tm, SSM_WIDTH), row),
        pl.BlockSpec((tm, ATTN_WIDTH), row),
        pl.BlockSpec((tm, D_MODEL), row),
        pl.BlockSpec((tm, D_MODEL), row),
        pl.BlockSpec((1, N_HEADS, QK_WIDTH, tm), lambda i: (i // seq_tiles, 0, 0, i % seq_tiles)),
        pl.BlockSpec((1, tm, QK_WIDTH), lambda i: (i // seq_tiles, i % seq_tiles, 0)),
        pl.BlockSpec((1, KV_LORA, tm), lambda i: (i // seq_tiles, 0, i % seq_tiles)),
        pl.BlockSpec((tm, D_MODEL), row),
    )
    return pl.pallas_call(
        functools.partial(_pre_kernel, seq_tiles=seq_tiles),
        grid=(n_tiles,),
        in_specs=[pl.BlockSpec((tm, D_MODEL), row)] + [full(a) for a in weights],
        out_specs=out_specs,
        out_shape=out_shape,
        scratch_shapes=[pltpu.VMEM((SSM_WIDTH // LANES, tm, LANES), F32),
                        pltpu.VMEM((2, tm, LANES), F32)],
        compiler_params=pltpu.CompilerParams(
            dimension_semantics=("arbitrary",), vmem_limit_bytes=VMEM_LIMIT_BYTES),
        name="pre",
    )(x2d, *weights)


def _cmul_add(ar, ai, xr, xi, yr, yi):
    return ar * xr - ai * xi + yr, ar * xi + ai * xr + yi


def _tile_scan(sr, si, cr, ci, pw, reverse):
    row = lax.broadcasted_iota(jnp.int32, (SUBLANES, LANES), 0)
    xr, xi = sr, si
    for k, s in enumerate((1, 2, 4)):
        shift = SUBLANES - s if reverse else s
        keep = (row < SUBLANES - s) if reverse else (row >= s)
        shr = jnp.where(keep, pltpu.roll(xr, shift, 0), 0.0)
        shi = jnp.where(keep, pltpu.roll(xi, shift, 0), 0.0)
        ar, ai = pw(SUBLANES + k)
        xr, xi = _cmul_add(ar, ai, shr, shi, xr, xi)
    pr, pi = pw(slice(0, SUBLANES))
    hr, hi = _cmul_add(pr, pi, cr, ci, xr, xi)
    edge = 0 if reverse else SUBLANES - 1
    out_r = jnp.broadcast_to(hr[edge:edge + 1], (SUBLANES, LANES))
    out_i = jnp.broadcast_to(hi[edge:edge + 1], (SUBLANES, LANES))
    first = (row == SUBLANES - 1) if reverse else (row == 0)
    one = SUBLANES - 1 if reverse else 1
    in_r = jnp.where(first, cr, pltpu.roll(hr, one, 0))
    in_i = jnp.where(first, ci, pltpu.roll(hi, one, 0))
    return in_r, in_i, out_r, out_i


def _s5_kernel(u_ref, t_ref, win_ref, wout_ref, pw_ref, y_ref, s_sc, h_sc, *, nb, nchunk):
    u = u_ref[0]
    s_sc[...] = _dot(u, win_ref[0])
    ntile = nchunk // SUBLANES

    def table(direction):
        base = direction * 2 * S5_PW_ROWS
        def pw(k):
            k = slice(k, k + 1) if isinstance(k, int) else k
            re = pw_ref[0, base + k.start:base + k.stop, :]
            im = pw_ref[0, base + S5_PW_ROWS + k.start:base + S5_PW_ROWS + k.stop, :]
            return re, im
        return pw

    pw_f, pw_b = table(0), table(1)

    def step(t, carry):
        new = []
        for b in range(nb):
            cfr, cfi, cbr, cbi = carry[b]
            rf = pl.multiple_of(b * nchunk + t * SUBLANES, SUBLANES)
            rb = pl.multiple_of(b * nchunk + (ntile - 1 - t) * SUBLANES, SUBLANES)
            rows_f, rows_b = pl.ds(rf, SUBLANES), pl.ds(rb, SUBLANES)
            in_r, in_i, cfr, cfi = _tile_scan(s_sc[rows_f, 0:LANES], s_sc[rows_f, LANES:2 * LANES],
                                              cfr, cfi, pw_f, False)
            h_sc[rows_f, 0:LANES] = in_r
            h_sc[rows_f, LANES:2 * LANES] = in_i
            in_r, in_i, cbr, cbi = _tile_scan(s_sc[rows_b, 2 * LANES:3 * LANES], s_sc[rows_b, 3 * LANES:4 * LANES],
                                              cbr, cbi, pw_b, True)
            h_sc[rows_b, 2 * LANES:3 * LANES] = in_r
            h_sc[rows_b, 3 * LANES:4 * LANES] = in_i
            new.append((cfr, cfi, cbr, cbi))
        return tuple(new)

    zero = jnp.zeros((SUBLANES, LANES), F32)
    lax.fori_loop(0, ntile, step, tuple((zero, zero, zero, zero) for _ in range(nb)))
    y_ref[0] = _dot(u, t_ref[0]) + _dot(h_sc[...].astype(BF16), wout_ref[0])


def _s5_call(u2, nb, nchunk, w):
    npair, rows, width = u2.shape
    blk = lambda i: (i, 0, 0)
    return pl.pallas_call(
        functools.partial(_s5_kernel, nb=nb, nchunk=nchunk),
        grid=(npair,),
        in_specs=[pl.BlockSpec((1, rows, width), blk),
                  pl.BlockSpec((1, width, width), blk),
                  pl.BlockSpec((1, width, width), blk),
                  pl.BlockSpec((1, width, width), blk),
                  pl.BlockSpec((1, 4 * S5_PW_ROWS, LANES), blk)],
        out_specs=pl.BlockSpec((1, rows, width), blk),
        out_shape=jax.ShapeDtypeStruct((npair, rows, width), F32),
        scratch_shapes=[pltpu.VMEM((rows, width), F32), pltpu.VMEM((rows, width), F32)],
        compiler_params=pltpu.CompilerParams(
            dimension_semantics=("arbitrary",), vmem_limit_bytes=VMEM_LIMIT_BYTES),
        name="s5",
    )(u2, w['s5_t'], w['s5_win'], w['s5_wout'], w['s5_pw'])


def _row_blocks(x, op):
    n = x.shape[0] // SUBLANES
    accs = [x[r * SUBLANES:(r + 1) * SUBLANES] for r in range(min(4, n))]
    for r in range(len(accs), n):
        accs[r % 4] = op(accs[r % 4], x[r * SUBLANES:(r + 1) * SUBLANES])
    out = accs[0]
    for a in accs[1:]:
        out = op(out, a)
    return out


def _spread(v, n):
    return jnp.concatenate([v] * (n // SUBLANES), axis=0)


def _attn_kernel(qt_ref, k_ref, vt_ref, wuvt_ref, o_ref, m_sc, c_sc, l_sc, bad_sc, acc_sc, s_sc, *, tk):
    nh, width, tq = qt_ref.shape[1:]
    nk = k_ref.shape[1] // tk
    latent = vt_ref.shape[1]

    def key_rows(kt):
        return pl.ds(pl.multiple_of(kt * tk, tk), tk)

    def scores(kt):
        qt = jnp.concatenate([qt_ref[0, h] for h in range(nh)], axis=1)
        return _dot(k_ref[0, key_rows(kt), :], qt)

    def values(p, kt):
        return _dot(vt_ref[0, :, key_rows(kt)], p.astype(BF16))

    def reset():
        m_sc[...] = jnp.full(m_sc.shape, -1e30, F32)
        l_sc[...] = jnp.zeros(l_sc.shape, F32)
        acc_sc[...] = jnp.zeros(acc_sc.shape, F32)

    reset()
    c_sc[...] = jnp.zeros(c_sc.shape, F32)
    bad_sc[...] = jnp.full(bad_sc.shape, -1.0, F32)

    group = min(ATTN_TILES_PER_TRIP, nk)

    def fast_group(i, carry):
        c = c_sc[...]
        c_wide = _spread(c, tk)
        l_new, acc_new, m_grp = l_sc[...], acc_sc[...], None
        for j in range(group):
            kt = group * i + j
            s = scores(kt)
            p = jnp.exp2(s - c_wide)
            blk_max = _row_blocks(s, jnp.maximum)
            m_grp = blk_max if m_grp is None else jnp.maximum(m_grp, blk_max)
            l_new = l_new + _row_blocks(p, jnp.add)
            acc_new = acc_new + values(p, kt)
        m_next = jnp.maximum(m_sc[...], jnp.max(m_grp, axis=0, keepdims=True))
        shift = m_next - c
        bad_sc[...] = jnp.maximum(bad_sc[...], jnp.abs(shift) - ATTN_OFFSET_SLACK)
        alpha = jnp.exp2(-shift)
        l_sc[...] = alpha * l_new
        acc_sc[...] = _spread(alpha, latent) * acc_new
        m_sc[...] = m_next
        c_sc[...] = m_next
        return carry

    lax.fori_loop(0, nk // group, fast_group, 0)

    def absorb(kt, carry):
        s_sc[...] = scores(kt)
        s = s_sc[...]
        m_prev = m_sc[...]
        m_next = jnp.maximum(m_prev, jnp.max(_row_blocks(s, jnp.maximum), axis=0, keepdims=True))
        alpha = jnp.exp2(m_prev - m_next)
        p = jnp.exp2(s - _spread(m_next, tk))
        l_sc[...] = alpha * l_sc[...] + _row_blocks(p, jnp.add)
        acc_sc[...] = _spread(alpha, latent) * acc_sc[...] + values(p, kt)
        m_sc[...] = m_next
        return carry

    @pl.when(jnp.max(bad_sc[...]) > 0.0)
    def _():
        reset()
        lax.fori_loop(0, nk, absorb, 0)

    l_col = jnp.sum(l_sc[...], axis=0, keepdims=True)
    latt = (acc_sc[...] / l_col).astype(BF16)
    outs = [_dot(wuvt_ref[h], latt[:, h * tq:(h + 1) * tq]) for h in range(nh)]
    o_ref[0] = jnp.transpose(jnp.concatenate(outs, axis=0))


def _attn_call(qt, k, vt, wuvt):
    bsz, nh, width, seq = qt.shape
    tq = min(ATTN_TQ, seq)
    tk = min(ATTN_TK, seq // 2)
    assert seq % tk == 0 and (seq // tk) % min(ATTN_TILES_PER_TRIP, seq // tk) == 0 and seq % tq == 0
    cols = nh * tq
    stat = pltpu.VMEM((SUBLANES, cols), F32)
    return pl.pallas_call(
        functools.partial(_attn_kernel, tk=tk),
        grid=(bsz, seq // tq),
        in_specs=[pl.BlockSpec((1, nh, width, tq), lambda b, i: (b, 0, 0, i)),
                  pl.BlockSpec((1, seq, width), lambda b, i: (b, 0, 0)),
                  pl.BlockSpec((1, vt.shape[1], seq), lambda b, i: (b, 0, 0)),
                  pl.BlockSpec(wuvt.shape, lambda b, i: (0, 0, 0))],
        out_specs=pl.BlockSpec((1, tq, ATTN_WIDTH), lambda b, i: (b, i, 0)),
        out_shape=jax.ShapeDtypeStruct((bsz, seq, ATTN_WIDTH), F32),
        scratch_shapes=[stat, stat, stat, stat, pltpu.VMEM((vt.shape[1], cols), F32),
                        pltpu.VMEM((tk, cols), F32)],
        compiler_params=pltpu.CompilerParams(
            dimension_semantics=("arbitrary", "arbitrary"), vmem_limit_bytes=VMEM_LIMIT_BYTES),
        name="attn",
    )(qt, k, vt, wuvt)


def _post_kernel(xa_ref, y2_ref, zs_ref, at_ref, za_ref, gs_ref, ga_ref,
                 wglu_ref, bglu_ref, wbs_ref, wba_ref, wo_ref, lng_ref, lnb_ref, o_ref, y_sc):
    y = _to_token_major(y2_ref, y_sc)
    half = y.shape[0] // 2
    for r0 in (0, half):
        rows = slice(r0, r0 + half)
        ys = jax.nn.gelu(y[rows])
        ys = ys * _sigmoid(_dot(ys.astype(BF16), wglu_ref[...]) + bglu_ref[...])
        ys = ys * zs_ref[rows, :].astype(F32)
        ya = at_ref[rows, :] * za_ref[rows, :].astype(F32)
        merged = (gs_ref[rows, :].astype(F32) * _dot(ys.astype(BF16), wbs_ref[...])
                  + ga_ref[rows, :].astype(F32) * _dot(ya.astype(BF16), wba_ref[...]))
        r = xa_ref[rows, :] + _dot(merged.astype(BF16), wo_ref[...])
        o_ref[rows, :] = _layer_norm(r, lng_ref[...], lnb_ref[...])


def _post_call(xa, y2, zs, at, za, gs, ga, w):
    tokens = xa.shape[0]
    tm = min(PRE_TILE, tokens)
    row = lambda i: (i, 0)
    const = lambda i: (0, 0)
    full = lambda a: pl.BlockSpec(a.shape, const)
    acts = [xa, zs, at, za, gs, ga]
    weights = [w['w_glu'], w['b_glu'], w['w_bs'], w['w_ba'], w['w_o'], w['ln_g'], w['ln_b']]
    return pl.pallas_call(
        _post_kernel,
        grid=(tokens // tm,),
        in_specs=([pl.BlockSpec((tm, D_MODEL), row),
                   pl.BlockSpec((SSM_G // 2, tm // SSM_CHUNK, SSM_PAIR), lambda i: (0, i, 0))]
                  + [pl.BlockSpec((tm, a.shape[1]), row) for a in acts[1:]]
                  + [full(a) for a in weights]),
        out_specs=pl.BlockSpec((tm, D_MODEL), row),
        out_shape=jax.ShapeDtypeStruct((tokens, D_MODEL), F32),
        scratch_shapes=[pltpu.VMEM((SSM_WIDTH // LANES, tm, LANES), F32)],
        compiler_params=pltpu.CompilerParams(
            dimension_semantics=("arbitrary",), vmem_limit_bytes=VMEM_LIMIT_BYTES),
        name="post",
    )(xa, y2, *acts[1:], *weights)


def _s5_operators(p):
    q = SSM_CHUNK
    gp, n2, c2 = SSM_G // 2, 2 * SSM_N, PAIR_LANES
    w2 = q * c2
    k = jnp.arange(q + 1, dtype=F32)
    same = jnp.eye(2, dtype=F32)

    def pair_states(v):
        return v.reshape(gp, n2)

    def direction(a_re, a_im, log_dt, c_re, c_im):
        a_re, a_im = pair_states(a_re), pair_states(a_im)
        dt = jnp.repeat(jnp.exp(log_dt), SSM_N).reshape(gp, n2)
        er = jnp.exp(a_re * dt)
        lbr, lbi = er * jnp.cos(a_im * dt), er * jnp.sin(a_im * dt)
        den = a_re * a_re + a_im * a_im
        nr, ni = lbr - 1.0, lbi
        cfr = (nr * a_re + ni * a_im) / den
        cfi = (ni * a_re - nr * a_im) / den
        def embed(x):
            x = x.reshape(gp, 2, SSM_P, SSM_N)
            x = x[:, :, :, None, :] * same[None, :, None, :, None]
            return x.reshape(gp, c2, n2)
        b_re = embed(jnp.swapaxes(p['b_re'], 1, 2))
        b_im = embed(jnp.swapaxes(p['b_im'], 1, 2))
        btr = b_re * cfr[:, None, :] - b_im * cfi[:, None, :]
        bti = b_re * cfi[:, None, :] + b_im * cfr[:, None, :]
        cr, ci = embed(c_re), embed(c_im)
        mag = jnp.exp(k[:, None, None] * (a_re * dt)[None])
        ph = k[:, None, None] * (a_im * dt)[None]
        pwr, pwi = mag * jnp.cos(ph), mag * jnp.sin(ph)
        cpr = cr[None] * pwr[:, :, None, :] - ci[None] * pwi[:, :, None, :]
        cpi = cr[None] * pwi[:, :, None, :] + ci[None] * pwr[:, :, None, :]
        pbr = btr[None] * pwr[:, :, None, :] - bti[None] * pwi[:, :, None, :]
        pbi = btr[None] * pwi[:, :, None, :] + bti[None] * pwr[:, :, None, :]
        taps = (jnp.einsum('gcn,tgdn->tgcd', btr, cpr, precision=HIGHEST)
                - jnp.einsum('gcn,tgdn->tgcd', bti, cpi, precision=HIGHEST))
        return taps, (pbr, pbi), (cpr, cpi), (pwr[q], pwi[q])

    tf, pbf, cpf, af = direction(p['a_re_f'], p['a_im_f'], p['log_dt_f'], p['c_re_f'], p['c_im_f'])
    tb, pbb, cpb, ab = direction(p['a_re_b'], p['a_im_b'], p['log_dt_b'], p['c_re_b'], p['c_im_b'])

    skip = jnp.eye(c2, dtype=F32)[None] * p['d_skip'].reshape(gp, 1, c2)
    centre = tf[0] + tb[0] + skip
    lags = jnp.concatenate([tb[q - 1:0:-1], centre[None], tf[1:q], jnp.zeros_like(tf[:1])])
    lags = jnp.transpose(lags, (1, 2, 0, 3)).reshape(gp, c2, 2 * w2).astype(BF16)
    t2 = jnp.stack([lags[:, :, (q - 1 - j) * c2:(q - 1 - j) * c2 + w2] for j in range(q)], axis=1)

    win = jnp.concatenate([pbf[0][q - 1::-1], pbf[1][q - 1::-1], pbb[0][:q], pbb[1][:q]], axis=-1)
    win2 = jnp.transpose(win.astype(BF16), (1, 0, 2, 3))

    wout = jnp.concatenate([cpf[0][1:], -cpf[1][1:], cpb[0][q:0:-1], -cpb[1][q:0:-1]], axis=-1)
    wout2 = jnp.swapaxes(jnp.transpose(wout.astype(BF16), (1, 0, 2, 3)).reshape(gp, w2, 4 * n2), 1, 2)

    def powers(a, reverse):
        ar, ai = a
        pows = [(ar, ai)]
        for _ in range(SUBLANES - 1):
            pr, pi = pows[-1]
            pows.append((pr * ar - pi * ai, pr * ai + pi * ar))
        per_row = pows[::-1] if reverse else pows
        rows = per_row + [pows[0], pows[1], pows[3]]
        rows = rows + [(jnp.zeros_like(ar), jnp.zeros_like(ar))] * (S5_PW_ROWS - len(rows))
        return jnp.concatenate([jnp.stack([r[0] for r in rows], axis=1),
                                jnp.stack([r[1] for r in rows], axis=1)], axis=1)

    pw2 = jnp.concatenate([powers(af, False), powers(ab, True)], axis=1)
    return t2.reshape(gp, w2, w2), win2.reshape(gp, w2, 4 * n2), wout2, pw2


def _prep_weights(ln_in_g, ln_in_b, w_in, ssm_b_re, ssm_b_im, ssm_a_re_fwd, ssm_a_im_fwd,
                  ssm_log_dt_fwd, ssm_c_re_fwd, ssm_c_im_fwd, ssm_a_re_bwd, ssm_a_im_bwd,
                  ssm_log_dt_bwd, ssm_c_re_bwd, ssm_c_im_bwd, ssm_d, w_glu, b_glu, q_norm_g, w_uq,
                  kv_norm_g, w_ukv, w_branch_ssm, w_branch_attn, w_o, ln_g, ln_b):
    l = 0
    w = {}
    row = lambda v: v.reshape(1, -1).astype(F32)
    w['ln_in_g'], w['ln_in_b'] = row(ln_in_g), row(ln_in_b)
    splits = [0]
    for s in IN_SIZES:
        splits.append(splits[-1] + s)
    seg = lambda i: w_in[l][:, splits[i]:splits[i + 1]]
    w['w_u'], w['w_zs'] = seg(0).astype(BF16), seg(1).astype(BF16)
    w['w_cq'], w['w_ckv'] = seg(2).astype(BF16), seg(3).astype(BF16)
    kr = seg(4)
    pad = jnp.zeros((D_MODEL, LANES - QK_ROPE), F32)
    k1, k2 = kr[:, :HALF_ROPE], kr[:, HALF_ROPE:]
    w['w_kr'] = jnp.concatenate([k1, k2, pad, k2, k1, pad], axis=1).astype(BF16)
    w['w_za'], w['w_gs'], w['w_ga'] = seg(5).astype(BF16), seg(6).astype(BF16), seg(7).astype(BF16)

    wuq = w_uq[l].reshape(Q_LORA, N_HEADS, QK_NOPE + QK_ROPE)
    wukv = w_ukv[l].reshape(KV_LORA, N_HEADS, QK_NOPE + V_HEAD)
    w_lat = jnp.einsum('chd,khd->chk', wuq[:, :, :QK_NOPE], wukv[:, :, :QK_NOPE], precision=HIGHEST)
    r1 = wuq[:, :, QK_NOPE:QK_NOPE + HALF_ROPE]
    r2 = wuq[:, :, QK_NOPE + HALF_ROPE:]
    zpad = jnp.zeros((Q_LORA, N_HEADS, LANES - QK_ROPE), F32)
    qscale = (QK_NOPE + QK_ROPE) ** -0.5 * math.log2(math.e)
    wq = jnp.concatenate([w_lat, r1, r2, zpad, r2, r1, zpad], axis=2) * qscale
    w['w_q'] = wq.reshape(Q_LORA, N_HEADS * 3 * LANES).astype(BF16)
    w['w_uvt'] = jnp.transpose(wukv[:, :, QK_NOPE:], (1, 2, 0)).astype(BF16)
    w['q_norm_g'], w['kv_norm_g'] = row(q_norm_g[l]), row(kv_norm_g[l])

    inv = ROPE_THETA ** (-jnp.arange(HALF_ROPE, dtype=F32) * 2.0 / QK_ROPE)
    zl = jnp.zeros((LANES - QK_ROPE,), F32)
    w['rope_inv'] = jnp.concatenate([inv, inv, zl]).reshape(1, LANES)
    one = jnp.ones((HALF_ROPE,), F32)
    w['rope_sgn'] = jnp.concatenate([-one, one, zl]).reshape(1, LANES)

    p = dict(b_re=ssm_b_re[l], b_im=ssm_b_im[l], d_skip=ssm_d[l],
             a_re_f=ssm_a_re_fwd[l], a_im_f=ssm_a_im_fwd[l], log_dt_f=ssm_log_dt_fwd[l],
             c_re_f=ssm_c_re_fwd[l], c_im_f=ssm_c_im_fwd[l],
             a_re_b=ssm_a_re_bwd[l], a_im_b=ssm_a_im_bwd[l], log_dt_b=ssm_log_dt_bwd[l],
             c_re_b=ssm_c_re_bwd[l], c_im_b=ssm_c_im_bwd[l])
    w['s5_t'], w['s5_win'], w['s5_wout'], w['s5_pw'] = _s5_operators(p)

    w['w_glu'], w['b_glu'] = w_glu[l].astype(BF16), row(b_glu[l])
    w['w_bs'], w['w_ba'] = w_branch_ssm[l].astype(BF16), w_branch_attn[l].astype(BF16)
    w['w_o'] = w_o[l].astype(BF16)
    w['ln_g'], w['ln_b'] = row(ln_g[l]), row(ln_b[l])
    return w


def _trunk(x, w):
    bsz, seq, _ = x.shape
    tokens = bsz * seq
    x2d = x.reshape(tokens, D_MODEL)
    u2, zs, za, gs, ga, qt, k, vt, xa = _pre_call(x2d, bsz, seq, w)
    y2 = _s5_call(u2, bsz, seq // SSM_CHUNK, w)
    at = _attn_call(qt, k, vt, w['w_uvt']).reshape(tokens, ATTN_WIDTH)
    out = _post_call(xa, y2, zs, at, za, gs, ga, w)
    return out.reshape(bsz, seq, D_MODEL)


def kernel(x_prompt, x_sample, ln_in_g, ln_in_b, w_in, ssm_b_re, ssm_b_im, ssm_a_re_fwd, ssm_a_im_fwd, ssm_log_dt_fwd, ssm_c_re_fwd, ssm_c_im_fwd, ssm_a_re_bwd, ssm_a_im_bwd, ssm_log_dt_bwd, ssm_c_re_bwd, ssm_c_im_bwd, ssm_d, w_glu, b_glu, q_norm_g, w_uq, kv_norm_g, w_ukv, w_branch_ssm, w_branch_attn, w_o, ln_g, ln_b):
    w = _prep_weights(ln_in_g, ln_in_b, w_in, ssm_b_re, ssm_b_im, ssm_a_re_fwd, ssm_a_im_fwd,
                      ssm_log_dt_fwd, ssm_c_re_fwd, ssm_c_im_fwd, ssm_a_re_bwd, ssm_a_im_bwd,
                      ssm_log_dt_bwd, ssm_c_re_bwd, ssm_c_im_bwd, ssm_d, w_glu, b_glu, q_norm_g,
                      w_uq, kv_norm_g, w_ukv, w_branch_ssm, w_branch_attn, w_o, ln_g, ln_b)
    return (_trunk(x_prompt, w), _trunk(x_sample, w))
```
